```python
import jax, jax.numpy as jnp
from jax import lax
import numpy as np

D_MODEL = 1024
BATCH = 2
SEQ = 8192
DEPTH = 1

D_MIX = D_MODEL
HEAD_DIM = 64
MLSTM_HEADS = (D_MIX // 2) // HEAD_DIM
MOBA_HEADS = (D_MIX // 2) // HEAD_DIM
MLSTM_WIDTH = MLSTM_HEADS * HEAD_DIM
MOBA_WIDTH = MOBA_HEADS * HEAD_DIM
MLSTM_CHUNK = 128
CONV_WIDTH = 4
MOBA_BLOCK = 256
MOBA_TOPK = 3
MOBA_Q_CHUNK = 64
N_EXPERTS = 32
TOP_K = 4
D_FF = D_MODEL
SWIGLU_LIMIT = 7.0
SWIGLU_ALPHA = 1.702
PLE_DIM = 256
RMS_EPS = 1e-6
PROJ_COLS = 4 * MLSTM_WIDTH + 2 * MLSTM_HEADS + 3 * MOBA_WIDTH

kernel_name = "hymba_mlstm_moba_moe_block"


def rmsnorm(x, g):
    xf = x.astype(jnp.float32)
    y = xf * lax.rsqrt(jnp.mean(xf * xf, axis=-1, keepdims=True) + RMS_EPS)
    return (y * g.astype(jnp.float32)).astype(x.dtype)


def alibi_slopes(n_heads):
    return jnp.exp2(-8.0 * jnp.arange(1, n_heads + 1, dtype=jnp.float32) / n_heads)


def causal_conv_silu(u, w, b):
    K = w.shape[0]
    S = u.shape[1]
    up = jnp.pad(u, ((0, 0), (K - 1, 0), (0, 0)))
    y = b + up[:, 0:S] * w[0]
    for j in range(1, K):
        y = y + up[:, j:j + S] * w[j]
    return jax.nn.silu(y)


def mlstm_chunkwise(q, k, v, i_pre, f_pre):
    B, H, S, Dh = q.shape
    L = MLSTM_CHUNK
    NC = S // L
    lf = jax.nn.log_sigmoid(f_pre.astype(jnp.float32))

    def to_chunks(a):
        a = a.astype(jnp.float32)
        return jnp.moveaxis(a.reshape(a.shape[:2] + (NC, L) + a.shape[3:]), 2, 0)

    xs = (to_chunks(q), to_chunks(k), to_chunks(v), to_chunks(i_pre), to_chunks(lf))
    causal = jnp.tril(jnp.ones((L, L), dtype=bool))

    def step(carry, inp):
        C, n, m = carry
        q_, k_, v_, i_, f_ = inp
        b = jnp.cumsum(f_, axis=-1)
        logD = b[..., :, None] - b[..., None, :] + i_[..., None, :]
        logD = jnp.where(causal, logD, -jnp.inf)
        inter = b + m[..., None]
        m_t = jnp.maximum(inter, jnp.max(logD, axis=-1))
        Dm = jnp.exp(logD - m_t[..., None])
        sc_inter = jnp.exp(inter - m_t)
        s = jnp.einsum('bhtd,bhsd->bhts', q_, k_) * Dm
        num = jnp.einsum('bhts,bhsd->bhtd', s, v_) + sc_inter[..., None] * jnp.einsum('bhvk,bhtk->bhtv', C, q_)
        den = jnp.sum(s, axis=-1) + sc_inter * jnp.einsum('bhk,bhtk->bht', n, q_)
        h = num / jnp.maximum(jnp.abs(den), jnp.exp(-m_t))[..., None]
        bL = b[..., -1]
        logw = bL[..., None] - b + i_
        m_new = jnp.maximum(bL + m, jnp.max(logw, axis=-1))
        w = jnp.exp(logw - m_new[..., None])
        decay = jnp.exp(bL + m - m_new)
        C_new = decay[..., None, None] * C + jnp.einsum('bhs,bhsv,bhsk->bhvk', w, v_, k_)
        n_new = decay[..., None] * n + jnp.einsum('bhs,bhsk->bhk', w, k_)
        return (C_new, n_new, m_new), h

    init = (jnp.zeros((B, H, Dh, Dh), jnp.float32), jnp.zeros((B, H, Dh), jnp.float32),
            jnp.zeros((B, H), jnp.float32))
    _, hs = lax.scan(step, init, xs)
    return jnp.moveaxis(hs, 0, 2).reshape(B, H, S, Dh)


def moba_attention(q, k, v):
    B, H, S, Dh = q.shape
    BS = MOBA_BLOCK
    Qc = MOBA_Q_CHUNK
    S_pad = -(-S // BS) * BS
    q = q.astype(jnp.float32)
    pad = ((0, 0), (0, 0), (0, S_pad - S), (0, 0))
    k = jnp.pad(k.astype(jnp.float32), pad)
    v = jnp.pad(v.astype(jnp.float32), pad)
    NB = S_pad // BS
    n_sel = min(MOBA_TOPK, NB)
    kb = k.reshape(B, H, NB, BS, Dh)
    vb = v.reshape(B, H, NB, BS, Dh)
    kmean = jnp.mean(kb, axis=3)
    slopes = alibi_slopes(H)
    blk_ids = jnp.arange(NB)
    in_blk = jnp.arange(BS)
    gather = jax.vmap(jax.vmap(lambda blocks, idx: blocks[idx]))

    def one_chunk(c):
        t0 = c * Qc
        qc = lax.dynamic_slice_in_dim(q, t0, Qc, axis=2)
        pos_q = t0 + jnp.arange(Qc)
        own = t0 // BS
        gate = jnp.einsum('bhqd,bhnd->bhqn', qc, kmean)
        gate = jnp.where(blk_ids < own, gate, -jnp.inf)
        _, sel = lax.top_k(gate, n_sel)
        sel_valid = sel < own
        ks = gather(kb, sel)
        vs = gather(vb, sel)
        s_sel = jnp.einsum('bhqd,bhqjkd->bhqjk', qc, ks)
        pos_sel = sel[..., None] * BS + in_blk
        dist_sel = (pos_q[:, None, None] - pos_sel).astype(jnp.float32)
        s_sel = s_sel - slopes[:, None, None, None] * dist_sel
        s_sel = jnp.where(sel_valid[..., None], s_sel, -jnp.inf)
        k_own = lax.dynamic_index_in_dim(kb, own, axis=2, keepdims=False)
        v_own = lax.dynamic_index_in_dim(vb, own, axis=2, keepdims=False)
        s_own = jnp.einsum('bhqd,bhkd->bhqk', qc, k_own)
        dist_own = pos_q[:, None] - (own * BS + in_blk)[None, :]
        s_own = s_own - slopes[:, None, None] * dist_own.astype(jnp.float32)
        s_own = jnp.where(dist_own >= 0, s_own, -jnp.inf)
        scores = jnp.concatenate([s_sel.reshape(B, H, Qc, n_sel * BS), s_own], axis=-1)
        probs = jax.nn.softmax(scores, axis=-1)
        p_sel = probs[..., :n_sel * BS].reshape(B, H, Qc, n_sel, BS)
        p_own = probs[..., n_sel * BS:]
        return (jnp.einsum('bhqjk,bhqjkd->bhqd', p_sel, vs)
                + jnp.einsum('bhqk,bhkd->bhqd', p_own, v_own))

    outs = lax.map(one_chunk, jnp.arange(S // Qc))
    return jnp.moveaxis(outs, 0, 2).reshape(B, H, S, Dh)


def moe_ffn(h, w_router, b_router, w_gu, b_gu, w_dn, b_dn):
    B, S, D = h.shape
    t = h.reshape(B * S, D)
    logits = (t @ w_router + b_router).astype(jnp.float32)
    top_vals, top_idx = lax.top_k(logits, TOP_K)
    wts = jax.nn.softmax(top_vals, axis=-1)
    gates = jnp.sum(jax.nn.one_hot(top_idx, N_EXPERTS, dtype=jnp.float32) * wts[..., None], axis=1)
    y = jnp.zeros((B * S, D), jnp.float32)
    for e in range(N_EXPERTS):
        hu = t @ w_gu[e] + b_gu[e]
        glu = jnp.minimum(hu[:, :D_FF], SWIGLU_LIMIT)
        lin = jnp.clip(hu[:, D_FF:], -SWIGLU_LIMIT, SWIGLU_LIMIT)
        act = glu * jax.nn.sigmoid(SWIGLU_ALPHA * glu) * (lin + 1.0)
        y = y + gates[:, e:e + 1] * (act @ w_dn[e] + b_dn[e])
    return y.reshape(B, S, D)


def setup_inputs(seed: int = 0) -> dict:
    key = jax.random.key(seed)
    ks = jax.random.split(key, 24)
    L = DEPTH

    def nrm(k, shape, scale):
        return jax.random.normal(k, shape, jnp.float32) * scale

    return {
        "x": nrm(ks[0], (BATCH, SEQ, D_MODEL), 1.0),
        "p": nrm(ks[1], (DEPTH, BATCH, SEQ, PLE_DIM), 1.0),
        "mix_norm_g": 1.0 + nrm(ks[2], (L, D_MODEL), 0.02),
        "w_in": nrm(ks[3], (L, D_MODEL, PROJ_COLS), D_MODEL ** -0.5),
        "conv_w": nrm(ks[4], (L, CONV_WIDTH, 2 * MLSTM_WIDTH), CONV_WIDTH ** -0.5),
        "conv_b": nrm(ks[5], (L, 2 * MLSTM_WIDTH), 0.02),
        "igate_b": nrm(ks[6], (L, MLSTM_HEADS), 0.1),
        "fgate_b": jnp.linspace(3.0, 6.0, MLSTM_HEADS, dtype=jnp.float32) + nrm(ks[7], (L, MLSTM_HEADS), 0.1),
        "mlstm_head_g": 1.0 + nrm(ks[8], (L, MLSTM_HEADS, HEAD_DIM), 0.02),
        "q_norm_g": 1.0 + nrm(ks[9], (L, HEAD_DIM), 0.02),
        "k_norm_g": 1.0 + nrm(ks[10], (L, HEAD_DIM), 0.02),
        "w_out": nrm(ks[11], (L, D_MIX, D_MODEL), D_MIX ** -0.5),
        "ffn_norm_g": 1.0 + nrm(ks[12], (L, D_MODEL), 0.02),
        "w_router": nrm(ks[13], (L, D_MODEL, N_EXPERTS), D_MODEL ** -0.5),
        "b_router": nrm(ks[14], (L, N_EXPERTS), 0.01),
        "w_gate_up": nrm(ks[15], (L, N_EXPERTS, D_MODEL, 2 * D_FF), D_MODEL ** -0.5),
        "b_gate_up": nrm(ks[16], (L, N_EXPERTS, 2 * D_FF), 0.02),
        "w_down": nrm(ks[17], (L, N_EXPERTS, D_FF, D_MODEL), D_FF ** -0.5),
        "b_down": nrm(ks[18], (L, N_EXPERTS, D_MODEL), 0.02),
        "ple_norm_g": 1.0 + nrm(ks[19], (L, D_MODEL), 0.02),
        "w_ple_gate": nrm(ks[20], (L, D_MODEL, D_MODEL), D_MODEL ** -0.5),
        "w_ple_proj": nrm(ks[21], (L, PLE_DIM, D_MODEL), PLE_DIM ** -0.5),
    }


def reference(x, p, mix_norm_g, w_in, conv_w, conv_b, igate_b, fgate_b, mlstm_head_g,
              q_norm_g, k_norm_g, w_out, ffn_norm_g, w_router, b_router, w_gate_up,
              b_gate_up, w_down, b_down, ple_norm_g, w_ple_gate, w_ple_proj):
    B, S, _ = x.shape
    Wm, Hm, Wb, Hb = MLSTM_WIDTH, MLSTM_HEADS, MOBA_WIDTH, MOBA_HEADS
    splits = [2 * Wm, 3 * Wm, 4 * Wm, 4 * Wm + Hm, 4 * Wm + 2 * Hm,
              4 * Wm + 2 * Hm + Wb, 4 * Wm + 2 * Hm + 2 * Wb]

    def heads(a, n):
        return a.reshape(B, S, n, HEAD_DIM).transpose(0, 2, 1, 3)

    scale = HEAD_DIM ** -0.5
    for l in range(DEPTH):
        h = rmsnorm(x, mix_norm_g[l])
        u = h @ w_in[l]
        ml_qk, ml_v, ml_o, ml_i, ml_f, mb_q, mb_k, mb_v = jnp.split(u, splits, axis=-1)

        ml_qk = causal_conv_silu(ml_qk, conv_w[l], conv_b[l])
        q_m = heads(ml_qk[..., :Wm], Hm) * scale
        k_m = heads(ml_qk[..., Wm:], Hm)
        v_m = heads(ml_v, Hm)
        i_pre = (ml_i + igate_b[l]).transpose(0, 2, 1)
        f_pre = (ml_f + fgate_b[l]).transpose(0, 2, 1)
        h_m = mlstm_chunkwise(q_m, k_m, v_m, i_pre, f_pre)
        h_m = rmsnorm(h_m, mlstm_head_g[l][:, None, :])
        h_m = h_m.transpose(0, 2, 1, 3).reshape(B, S, Wm) * jax.nn.sigmoid(ml_o.astype(jnp.float32))

        q_b = rmsnorm(heads(mb_q, Hb), q_norm_g[l]) * scale
        k_b = rmsnorm(heads(mb_k, Hb), k_norm_g[l])
        v_b = heads(mb_v, Hb)
        h_b = moba_attention(q_b, k_b, v_b).transpose(0, 2, 1, 3).reshape(B, S, Wb)

        mix = jnp.concatenate([h_m, h_b], axis=-1).astype(x.dtype) @ w_out[l]
        x = x + mix.astype(x.dtype)

        y = moe_ffn(rmsnorm(x, ffn_norm_g[l]), w_router[l], b_router[l], w_gate_up[l],
                    b_gate_up[l], w_down[l], b_down[l])
        x = x + y.astype(x.dtype)

        g = jax.nn.sigmoid((rmsnorm(x, ple_norm_g[l]) @ w_ple_gate[l]).astype(jnp.float32))
        x = x + (g * (p[l] @ w_ple_proj[l]).astype(jnp.float32)).astype(x.dtype)
    return x
```

```python
import functools

import jax
import jax.numpy as jnp
from jax import lax
from jax.experimental import pallas as pl
from jax.experimental.pallas import tpu as pltpu

F32 = jnp.float32
BF16 = jnp.bfloat16
HIGHEST = lax.Precision.HIGHEST

HEAD_DIM = 64
LANES = 128
N_HEADS = 8
GROUP_WIDTH = N_HEADS * HEAD_DIM
N_PAIRS = N_HEADS // 2
MLSTM_CHUNK = 128
CONV_WIDTH = 4
MOBA_BLOCK = 256
MOBA_TOPK = 3
N_EXPERTS = 32
TOP_K = 4
SWIGLU_LIMIT = 7.0
SWIGLU_ALPHA = 1.702
RMS_EPS = 1e-6
NEG = -1e30
VMEM_LIMIT = 56 * 1024 * 1024

COL_QK = 0
COL_V = 2 * GROUP_WIDTH
COL_O = 3 * GROUP_WIDTH
COL_MQ = 4 * GROUP_WIDTH
COL_MK = 5 * GROUP_WIDTH
COL_MV = 6 * GROUP_WIDTH
MAIN_COLS = 7 * GROUP_WIDTH


def _cparams(sem):
    return pltpu.CompilerParams(dimension_semantics=sem, vmem_limit_bytes=VMEM_LIMIT)


def _dot(a, b, **kw):
    return jnp.dot(a, b, preferred_element_type=F32, **kw)


def _dot_nt(a, b, **kw):
    return lax.dot_general(a, b, (((1,), (1,)), ((), ())), preferred_element_type=F32, **kw)


def _sigmoid(x):
    return 1.0 / (1.0 + jnp.exp(-x))


def _lane_half_mask(shape):
    return lax.broadcasted_iota(jnp.int32, shape, len(shape) - 1) < HEAD_DIM


def _pair_block_ones():
    r = lax.broadcasted_iota(jnp.int32, (LANES, LANES), 0) // HEAD_DIM
    c = lax.broadcasted_iota(jnp.int32, (LANES, LANES), 1) // HEAD_DIM
    return r == c


def _in_proj_kernel(x_ref, g_ref, w_ref, wg_ref, u_ref, gates_ref, *, n_chunk):
    x = x_ref[...]
    h = x * lax.rsqrt(jnp.mean(x * x, axis=-1, keepdims=True) + RMS_EPS) * g_ref[...]
    hb = h.astype(BF16)
    for c in range(MAIN_COLS // n_chunk):
        u_ref[:, c * n_chunk:(c + 1) * n_chunk] = _dot(hb, w_ref[:, c * n_chunk:(c + 1) * n_chunk])
    gates_ref[...] = _dot(hb, wg_ref[...])


def _in_proj(x2, g, w_main, w_gates, tm=512):
    T, D = x2.shape
    return pl.pallas_call(
        functools.partial(_in_proj_kernel, n_chunk=GROUP_WIDTH),
        grid=(T // tm,),
        in_specs=[
            pl.BlockSpec((tm, D), lambda i: (i, 0)),
            pl.BlockSpec((1, D), lambda i: (0, 0)),
            pl.BlockSpec((D, MAIN_COLS), lambda i: (0, 0)),
            pl.BlockSpec((D, LANES), lambda i: (0, 0)),
        ],
        out_specs=[
            pl.BlockSpec((tm, MAIN_COLS), lambda i: (i, 0)),
            pl.BlockSpec((tm, LANES), lambda i: (i, 0)),
        ],
        out_shape=[
            jax.ShapeDtypeStruct((T, MAIN_COLS), F32),
            jax.ShapeDtypeStruct((T, LANES), F32),
        ],
        compiler_params=_cparams(("parallel",)),
        name="in_proj",
    )(x2, g, w_main, w_gates)


def _mlstm_kernel(qk_ref, v_ref, o_ref, gates_ref, cw_ref, cb_ref, gb_ref, hg_ref,
                  out_ref, ext_ref, c_state, n_state, m_state):
    L = MLSTM_CHUNK
    W = GROUP_WIDTH
    chunk = pl.program_id(1)

    @pl.when(chunk == 0)
    def _():
        ext_ref[0:8, :] = jnp.zeros((8, 2 * W), F32)
        c_state[...] = jnp.zeros_like(c_state)
        n_state[...] = jnp.zeros_like(n_state)
        m_state[...] = jnp.zeros_like(m_state)

    ext_ref[8:8 + L, :] = qk_ref[0]
    y = cb_ref[...] + ext_ref[8:8 + L, :] * cw_ref[3:4, :]
    for j in range(1, CONV_WIDTH):
        y = y + ext_ref[8 - j:8 - j + L, :] * cw_ref[3 - j:4 - j, :]
    ext_ref[0:8, :] = ext_ref[L:L + 8, :]
    a = y * _sigmoid(y)
    q_all = a[:, :W] * (HEAD_DIM ** -0.5)
    k_all = a[:, W:]

    pre = gates_ref[0] + gb_ref[...]
    lane = lax.broadcasted_iota(jnp.int32, (L, LANES), 1)
    logsig = jnp.minimum(pre, 0.0) - jnp.log(1.0 + jnp.exp(-jnp.abs(pre)))
    G = jnp.where(lane < N_HEADS, pre, logsig)
    row = lax.broadcasted_iota(jnp.int32, (L, L), 0)
    col = lax.broadcasted_iota(jnp.int32, (L, L), 1)
    causal = col <= row
    Bc = _dot(causal.astype(F32), G, precision=HIGHEST)
    GT = G.T
    BT = Bc.T

    first_half = _lane_half_mask((L, LANES))
    first_half_row = _lane_half_mask((1, LANES))
    blockdiag = _pair_block_ones()
    ones_blk = blockdiag.astype(F32)

    for p in range(N_PAIRS):
        sl = slice(p * LANES, (p + 1) * LANES)
        q_pair = q_all[:, sl]
        k_pair = k_all[:, sl]
        v_pair = v_ref[0][:, sl]
        qb = q_pair.astype(BF16)
        kb = k_pair.astype(BF16)
        vb = v_pair.astype(BF16)
        Cp = c_state[p]
        Np = n_state[p]
        q_c = _dot(qb, Cp.astype(BF16))
        q_n = _dot(q_pair, Np, precision=HIGHEST)
        h_halves = []
        w_cols = []
        decays = []
        for hh in range(2):
            h = 2 * p + hh
            bcol = Bc[:, N_HEADS + h:N_HEADS + h + 1]
            brow = BT[N_HEADS + h:N_HEADS + h + 1, :]
            irow = GT[h:h + 1, :]
            icol = G[:, h:h + 1]
            m_prev = m_state[h:h + 1, 0:1]
            logD = jnp.where(causal, bcol - brow + irow, -jnp.inf)
            inter = bcol + m_prev
            m_t = jnp.maximum(inter, jnp.max(logD, axis=-1, keepdims=True))
            Dm = jnp.exp(logD - m_t)
            sc = jnp.exp(inter - m_t)
            in_head = first_half if hh == 0 else jnp.logical_not(first_half)
            q_h = jnp.where(in_head, q_pair, 0.0).astype(BF16)
            s = _dot_nt(q_h, kb) * Dm
            num = _dot(s.astype(BF16), vb) + sc * q_c
            den = jnp.sum(s, axis=-1, keepdims=True) + sc * q_n
            h_halves.append(num / jnp.maximum(jnp.abs(den), jnp.exp(-m_t)))
            bL = bcol[L - 1:L, :]
            logw = bL - bcol + icol
            m_new = jnp.maximum(bL + m_prev, jnp.max(logw, axis=0, keepdims=True))
            w_cols.append(jnp.exp(logw - m_new))
            decays.append(jnp.exp(bL + m_prev - m_new))
            m_state[h:h + 1, :] = jnp.broadcast_to(m_new, (1, LANES))
        h_pair = jnp.where(first_half, h_halves[0], h_halves[1])
        wk = jnp.where(first_half, w_cols[0], w_cols[1]) * k_pair
        wkT = wk.T
        dec = jnp.where(first_half_row, decays[0], decays[1])
        c_state[p] = dec * Cp + jnp.where(blockdiag, _dot(wkT.astype(BF16), vb), 0.0)
        n_state[p] = dec * Np + jnp.where(blockdiag, jnp.sum(wkT, axis=-1, keepdims=True), 0.0)
        ms = _dot(h_pair * h_pair, ones_blk, precision=HIGHEST) * (1.0 / HEAD_DIM)
        hn = h_pair * lax.rsqrt(ms + RMS_EPS) * hg_ref[:, sl]
        out_ref[0, :, sl] = (hn * _sigmoid(o_ref[0][:, sl])).astype(out_ref.dtype)


def _mlstm(u3, gates3, conv_w, conv_b, gate_b, head_g):
    B, S, _ = u3.shape
    L = MLSTM_CHUNK
    W = GROUP_WIDTH
    return pl.pallas_call(
        _mlstm_kernel,
        grid=(B, S // L),
        in_specs=[
            pl.BlockSpec((1, L, 2 * W), lambda b, c: (b, c, COL_QK // (2 * W))),
            pl.BlockSpec((1, L, W), lambda b, c: (b, c, COL_V // W)),
            pl.BlockSpec((1, L, W), lambda b, c: (b, c, COL_O // W)),
            pl.BlockSpec((1, L, LANES), lambda b, c: (b, c, 0)),
            pl.BlockSpec((CONV_WIDTH, 2 * W), lambda b, c: (0, 0)),
            pl.BlockSpec((1, 2 * W), lambda b, c: (0, 0)),
            pl.BlockSpec((1, LANES), lambda b, c: (0, 0)),
            pl.BlockSpec((1, W), lambda b, c: (0, 0)),
        ],
        out_specs=pl.BlockSpec((1, L, W), lambda b, c: (b, c, 0)),
        out_shape=jax.ShapeDtypeStruct((B, S, W), BF16),
        scratch_shapes=[
            pltpu.VMEM((L + 8, 2 * W), F32),
            pltpu.VMEM((N_PAIRS, LANES, LANES), F32),
            pltpu.VMEM((N_PAIRS, LANES, LANES), F32),
            pltpu.VMEM((N_HEADS, LANES), F32),
        ],
        compiler_params=_cparams(("parallel", "arbitrary")),
        name="mlstm",
    )(u3, u3, u3, gates3, conv_w, conv_b, gate_b, head_g)


def _moba_kernel(q_ref, k_ref, v_ref, qg_ref, kg_ref, out_ref, kaug, vaug, kmt, *, n_blocks):
    BS = MOBA_BLOCK
    pair = pl.program_id(1)
    own = pl.program_id(2)

    @pl.when(own == 0)
    def _():
        kmt[...] = jnp.zeros_like(kmt)

    lane = lax.broadcasted_iota(jnp.int32, (BS, LANES), 1)
    rowi = lax.broadcasted_iota(jnp.int32, (BS, LANES), 0)
    ones_blk = _pair_block_ones().astype(F32)
    q_pair = q_ref[0]
    k_pair = k_ref[0]
    v_pair = v_ref[0]
    q_ms = _dot(q_pair * q_pair, ones_blk, precision=HIGHEST) * (1.0 / HEAD_DIM)
    k_ms = _dot(k_pair * k_pair, ones_blk, precision=HIGHEST) * (1.0 / HEAD_DIM)
    qn = q_pair * lax.rsqrt(q_ms + RMS_EPS) * qg_ref[...] * (HEAD_DIM ** -0.5)
    kn = k_pair * lax.rsqrt(k_ms + RMS_EPS) * kg_ref[...]
    own_off = pl.multiple_of(own * BS, BS)
    r2 = lax.broadcasted_iota(jnp.int32, (BS, BS), 0)
    c2 = lax.broadcasted_iota(jnp.int32, (BS, BS), 1)
    causal = c2 <= r2

    outs = []
    for hh in range(2):
        head = 2 * pair + hh
        slope = jnp.exp2(-jnp.full((1, 1), 1.0, F32) * (head + 1).astype(F32))
        base = HEAD_DIM if hh == 0 else 0
        in_head = (lane < HEAD_DIM) if hh == 0 else (lane >= HEAD_DIM)
        jidx = lane - (base + 2)
        q_h = jnp.where(in_head, qn, 0.0)
        k_h = jnp.where(in_head, kn, 0.0)

        gate = _dot(q_h, kmt[hh], precision=HIGHEST)
        g = jnp.where((jidx >= 0) & (jidx < own), gate, -jnp.inf)
        lane_f = lane.astype(F32)
        sel = jidx == own
        for _ in range(MOBA_TOPK):
            mx = jnp.max(g, axis=-1, keepdims=True)
            first = jnp.min(jnp.where(g == mx, lane_f, 1e9), axis=-1, keepdims=True)
            pick = (lane_f == first) & (mx > -jnp.inf)
            sel = sel | pick
            g = jnp.where(pick, -jnp.inf, g)
        in_gate = (jidx >= 0) & (jidx < n_blocks)
        q_aux = jnp.where(lane == base, 1.0, jnp.where(in_gate & jnp.logical_not(sel), NEG, 0.0))
        q_aug = jnp.where(in_head, q_h, q_aux).astype(BF16)

        k_aux = jnp.where(lane == base, slope * rowi.astype(F32),
                          jnp.where(jidx == own, 1.0, 0.0))
        k_aug = jnp.where(in_head, k_h, k_aux).astype(BF16)
        v_aug = jnp.where(in_head, v_pair, jnp.where(lane == base, 1.0, 0.0)).astype(BF16)
        kaug[hh, pl.ds(own_off, BS), :] = k_aug
        vaug[hh, pl.ds(own_off, BS), :] = v_aug
        tgt = jnp.where(jidx == own, 1.0 / BS, 0.0)
        kmt[hh] = kmt[hh] + _dot(k_h.T, tgt, precision=HIGHEST)

        s = jnp.where(causal, _dot_nt(q_aug, k_aug), NEG)
        m = jnp.max(s, axis=-1, keepdims=True)
        p = jnp.exp(s - m)
        acc = _dot(p.astype(BF16), v_aug)

        def body(j, carry):
            m, acc = carry
            off = pl.multiple_of(j * BS, BS)
            kj = kaug[hh, pl.ds(off, BS), :]
            vj = vaug[hh, pl.ds(off, BS), :]
            cj = slope * ((j - own) * BS).astype(F32)
            s = _dot_nt(q_aug, kj)
            m_new = jnp.maximum(m, jnp.max(s, axis=-1, keepdims=True) + cj)
            p = jnp.exp(s - (m_new - cj))
            acc = jnp.exp(m - m_new) * acc + _dot(p.astype(BF16), vj)
            return m_new, acc

        m, acc = lax.fori_loop(0, own, body, (m, acc))
        outs.append(acc / acc[:, base:base + 1])
    out_ref[0] = jnp.where(lane < HEAD_DIM, outs[0], outs[1]).astype(out_ref.dtype)


def _moba(u3, q_g, k_g):
    B, S, _ = u3.shape
    BS = MOBA_BLOCK
    nb = S // BS
    return pl.pallas_call(
        functools.partial(_moba_kernel, n_blocks=nb),
        grid=(B, N_PAIRS, nb),
        in_specs=[
            pl.BlockSpec((1, BS, LANES), lambda b, p, i: (b, i, COL_MQ // LANES + p)),
            pl.BlockSpec((1, BS, LANES), lambda b, p, i: (b, i, COL_MK // LANES + p)),
            pl.BlockSpec((1, BS, LANES), lambda b, p, i: (b, i, COL_MV // LANES + p)),
            pl.BlockSpec((1, LANES), lambda b, p, i: (0, 0)),
            pl.BlockSpec((1, LANES), lambda b, p, i: (0, 0)),
        ],
        out_specs=pl.BlockSpec((1, BS, LANES), lambda b, p, i: (b, i, p)),
        out_shape=jax.ShapeDtypeStruct((B, S, GROUP_WIDTH), BF16),
        scratch_shapes=[
            pltpu.VMEM((2, S, LANES), BF16),
            pltpu.VMEM((2, S, LANES), BF16),
            pltpu.VMEM((2, LANES, LANES), F32),
        ],
        compiler_params=_cparams(("parallel", "parallel", "arbitrary")),
        name="moba",
    )(u3, u3, u3, q_g, k_g)


def _mixing(x, mix_norm_g, w_in, conv_w, conv_b, igate_b, fgate_b, mlstm_head_g, q_norm_g, k_norm_g):
    B, S, D = x.shape
    W = GROUP_WIDTH
    n_gate = 2 * N_HEADS
    w_main = jnp.concatenate([w_in[:, :4 * W], w_in[:, 4 * W + n_gate:]], axis=1).astype(BF16)
    w_gates = jnp.pad(w_in[:, 4 * W:4 * W + n_gate], ((0, 0), (0, LANES - n_gate))).astype(BF16)
    u, gates = _in_proj(x.reshape(B * S, D), mix_norm_g.reshape(1, D), w_main, w_gates)
    u3 = u.reshape(B, S, MAIN_COLS)
    gates3 = gates.reshape(B, S, LANES)
    gate_b = jnp.pad(jnp.concatenate([igate_b, fgate_b]), (0, LANES - n_gate)).reshape(1, LANES)
    h_m = _mlstm(u3, gates3, conv_w, conv_b.reshape(1, 2 * W), gate_b, mlstm_head_g.reshape(1, W))
    qg2 = jnp.concatenate([q_norm_g, q_norm_g]).reshape(1, LANES)
    kg2 = jnp.concatenate([k_norm_g, k_norm_g]).reshape(1, LANES)
    h_b = _moba(u3, qg2, kg2)
    return h_m, h_b


TOK_TILE = 256
EXPERT_TILE = 256


def _out_proj_kernel(x_ref, hm_ref, hb_ref, wo_ref, g_ref, wr_ref, br_ref,
                     x1_ref, hn_ref, route_ref, wts_ref, counts_ref, carry):
    tm = TOK_TILE
    W = GROUP_WIDTH
    step = pl.program_id(0)

    @pl.when(step == 0)
    def _():
        carry[...] = jnp.zeros_like(carry)

    x1 = x_ref[...] + _dot(hm_ref[...], wo_ref[0:W, :]) + _dot(hb_ref[...], wo_ref[W:2 * W, :])
    x1_ref[...] = x1
    hn = x1 * lax.rsqrt(jnp.mean(x1 * x1, axis=-1, keepdims=True) + RMS_EPS) * g_ref[...]
    hn_ref[...] = hn

    logits = _dot_nt(wr_ref[...], hn, precision=HIGHEST) + br_ref[:, 0:1]
    eidx = lax.broadcasted_iota(jnp.int32, (N_EXPERTS, tm), 0).astype(F32)
    vals, onehots, idxs = [], [], []
    l = logits
    for _ in range(TOP_K):
        mx = jnp.max(l, axis=0, keepdims=True)
        first = jnp.min(jnp.where(l == mx, eidx, 1e9), axis=0, keepdims=True)
        pick = eidx == first
        vals.append(mx)
        idxs.append(first)
        onehots.append(pick.astype(F32))
        l = jnp.where(pick, -jnp.inf, l)
    exps = [jnp.exp(v - vals[0]) for v in vals]
    inv = 1.0 / (exps[0] + exps[1] + exps[2] + exps[3])
    total = onehots[0] + onehots[1] + onehots[2] + onehots[3]
    r2 = lax.broadcasted_iota(jnp.int32, (tm, tm), 0)
    c2 = lax.broadcasted_iota(jnp.int32, (tm, tm), 1)
    incl = (r2 <= c2).astype(BF16)
    cum = _dot(total.astype(BF16), incl)
    excl = cum - total + carry[:, 0:1]
    ranks = [jnp.sum(oh * excl, axis=0, keepdims=True) for oh in onehots]
    carry[...] = carry[...] + jnp.sum(total, axis=-1, keepdims=True)
    counts_ref[...] = carry[...].astype(jnp.int32)

    row8 = lax.broadcasted_iota(jnp.int32, (8, tm), 0)
    route = jnp.zeros((8, tm), F32)
    for k in range(TOP_K):
        route = jnp.where(row8 == k, idxs[k], route)
        route = jnp.where(row8 == TOP_K + k, ranks[k], route)
    route_ref[...] = route.astype(jnp.int32)
    rowl = lax.broadcasted_iota(jnp.int32, (LANES, tm), 0)
    wt = jnp.zeros((LANES, tm), F32)
    for k in range(TOP_K):
        wt = jnp.where(rowl == k, exps[k] * inv, wt)
    wts_ref[...] = wt.T


def _out_proj(x2, h_m, h_b, w_out, g, w_router_t, b_router):
    T, D = x2.shape
    tm = TOK_TILE
    W = GROUP_WIDTH
    return pl.pallas_call(
        _out_proj_kernel,
        grid=(T // tm,),
        in_specs=[
            pl.BlockSpec((tm, D), lambda i: (i, 0)),
            pl.BlockSpec((tm, W), lambda i: (i, 0)),
            pl.BlockSpec((tm, W), lambda i: (i, 0)),
            pl.BlockSpec((2 * W, D), lambda i: (0, 0)),
            pl.BlockSpec((1, D), lambda i: (0, 0)),
            pl.BlockSpec((N_EXPERTS, D), lambda i: (0, 0)),
            pl.BlockSpec((N_EXPERTS, LANES), lambda i: (0, 0)),
        ],
        out_specs=[
            pl.BlockSpec((tm, D), lambda i: (i, 0)),
            pl.BlockSpec((tm, D), lambda i: (i, 0)),
            pl.BlockSpec((8, tm), lambda i: (0, i)),
            pl.BlockSpec((tm, LANES), lambda i: (i, 0)),
            pl.BlockSpec((N_EXPERTS, LANES), lambda i: (0, 0)),
        ],
        out_shape=[
            jax.ShapeDtypeStruct((T, D), F32),
            jax.ShapeDtypeStruct((T, D), F32),
            jax.ShapeDtypeStruct((8, T), jnp.int32),
            jax.ShapeDtypeStruct((T, LANES), F32),
            jax.ShapeDtypeStruct((N_EXPERTS, LANES), jnp.int32),
        ],
        scratch_shapes=[pltpu.VMEM((N_EXPERTS, LANES), F32)],
        compiler_params=_cparams(("arbitrary",)),
        name="out_proj",
    )(x2, h_m, h_b, w_out, g, w_router_t, b_router)


def _row_copy(src, s_row, dst, d_row, sem):
    return pltpu.make_async_copy(src.at[pl.ds(s_row, 1), :], dst.at[pl.ds(d_row, 1), :], sem)


def _dispatch_kernel(hn_ref, pos_hbm, xs_in, xs_hbm, pos_smem, pos_sem, row_sem):
    del xs_in
    tm = TOK_TILE
    step = pl.program_id(0)
    cp = pltpu.make_async_copy(pos_hbm.at[step], pos_smem, pos_sem)
    cp.start()
    cp.wait()

    def issue(t, _):
        for k in range(TOP_K):
            _row_copy(hn_ref, t, xs_hbm, pos_smem[k * tm + t], row_sem).start()
        return 0

    lax.fori_loop(0, tm, issue, 0)

    def drain(t, _):
        for k in range(TOP_K):
            _row_copy(hn_ref, 0, xs_hbm, 0, row_sem).wait()
        return 0

    lax.fori_loop(0, tm, drain, 0)


def _dispatch(hn, pos_tiles, xs_init):
    T, D = hn.shape
    tm = TOK_TILE
    return pl.pallas_call(
        _dispatch_kernel,
        grid=(T // tm,),
        in_specs=[
            pl.BlockSpec((tm, D), lambda i: (i, 0)),
            pl.BlockSpec(memory_space=pl.ANY),
            pl.BlockSpec(memory_space=pl.ANY),
        ],
        out_specs=pl.BlockSpec(memory_space=pl.ANY),
        out_shape=jax.ShapeDtypeStruct(xs_init.shape, F32),
        scratch_shapes=[
            pltpu.SMEM((TOP_K * tm,), jnp.int32),
            pltpu.SemaphoreType.DMA,
            pltpu.SemaphoreType.DMA,
        ],
        input_output_aliases={2: 0},
        compiler_params=_cparams(("arbitrary",)),
        name="dispatch",
    )(hn, pos_tiles, xs_init)


def _experts_kernel(te_ref, nu_ref, x_ref, wgu_ref, bgu_ref, wdn_ref, bdn_ref, y_ref, wgu_b, wdn_b):
    i = pl.program_id(0)
    F = wdn_ref.shape[1]
    prev = te_ref[jnp.maximum(i - 1, 0)]
    fresh = (i == 0) | (te_ref[i] != prev)

    @pl.when(fresh & (i < nu_ref[0]))
    def _():
        wgu_b[...] = wgu_ref[0].astype(BF16)
        wdn_b[...] = wdn_ref[0].astype(BF16)

    @pl.when(i < nu_ref[0])
    def _():
        xb = x_ref[...].astype(BF16)
        hu = _dot(xb, wgu_b[...]) + bgu_ref[0]
        glu = jnp.minimum(hu[:, :F], SWIGLU_LIMIT)
        lin = jnp.clip(hu[:, F:], -SWIGLU_LIMIT, SWIGLU_LIMIT)
        act = glu * _sigmoid(SWIGLU_ALPHA * glu) * (lin + 1.0)
        y_ref[...] = _dot(act.astype(BF16), wdn_b[...]) + bdn_ref[0]

    @pl.when(i >= nu_ref[0])
    def _():
        y_ref[...] = jnp.zeros_like(y_ref)


def _experts(tile_expert, n_used, x_sorted, w_gu, b_gu, w_dn, b_dn):
    P, D = x_sorted.shape
    E, _, F2 = w_gu.shape
    F = F2 // 2
    tm = EXPERT_TILE
    n_tiles = P // tm

    def row_map(i, te, nu):
        return (jnp.minimum(i, nu[0] - 1), 0)

    def exp_map(i, te, nu):
        return (te[i], 0, 0)

    grid_spec = pltpu.PrefetchScalarGridSpec(
        num_scalar_prefetch=2,
        grid=(n_tiles,),
        in_specs=[
            pl.BlockSpec((tm, D), row_map),
            pl.BlockSpec((1, D, F2), exp_map),
            pl.BlockSpec((1, 1, F2), exp_map),
            pl.BlockSpec((1, F, D), exp_map),
            pl.BlockSpec((1, 1, D), exp_map),
        ],
        out_specs=pl.BlockSpec((tm, D), lambda i, te, nu: (i, 0)),
        scratch_shapes=[pltpu.VMEM((D, F2), BF16), pltpu.VMEM((F, D), BF16)],
    )
    return pl.pallas_call(
        _experts_kernel,
        grid_spec=grid_spec,
        out_shape=jax.ShapeDtypeStruct((P, D), F32),
        compiler_params=_cparams(("arbitrary",)),
        name="experts",
    )(tile_expert, n_used, x_sorted, w_gu, b_gu.reshape(E, 1, F2), w_dn, b_dn.reshape(E, 1, D))


def _combine_kernel(x1_ref, wts_ref, p_ref, g_ref, wg_ref, wp_ref, pos_hbm, ys_hbm,
                    out_ref, ybuf, pos_smem, pos_sem, row_sem):
    tm = TOK_TILE
    step = pl.program_id(0)
    cp = pltpu.make_async_copy(pos_hbm.at[step], pos_smem, pos_sem)
    cp.start()
    cp.wait()

    def issue(t, _):
        for k in range(TOP_K):
            _row_copy(ys_hbm, pos_smem[k * tm + t], ybuf.at[k], t, row_sem).start()
        return 0

    lax.fori_loop(0, tm, issue, 0)

    def drain(t, _):
        for k in range(TOP_K):
            _row_copy(ys_hbm, 0, ybuf.at[k], 0, row_sem).wait()
        return 0

    lax.fori_loop(0, tm, drain, 0)

    w = wts_ref[...]
    x2 = x1_ref[...]
    for k in range(TOP_K):
        x2 = x2 + w[:, k:k + 1] * ybuf[k]
    hn = x2 * lax.rsqrt(jnp.mean(x2 * x2, axis=-1, keepdims=True) + RMS_EPS) * g_ref[...]
    gate = _sigmoid(_dot(hn.astype(BF16), wg_ref[...]))
    out_ref[...] = x2 + gate * _dot(p_ref[...].astype(BF16), wp_ref[...])


def _combine(x1, wts, p2, g, w_gate, w_proj, pos_tiles, y_sorted):
    T, D = x1.shape
    tm = TOK_TILE
    PD = p2.shape[1]
    return pl.pallas_call(
        _combine_kernel,
        grid=(T // tm,),
        in_specs=[
            pl.BlockSpec((tm, D), lambda i: (i, 0)),
            pl.BlockSpec((tm, LANES), lambda i: (i, 0)),
            pl.BlockSpec((tm, PD), lambda i: (i, 0)),
            pl.BlockSpec((1, D), lambda i: (0, 0)),
            pl.BlockSpec((D, D), lambda i: (0, 0)),
            pl.BlockSpec((PD, D), lambda i: (0, 0)),
            pl.BlockSpec(memory_space=pl.ANY),
            pl.BlockSpec(memory_space=pl.ANY),
        ],
        out_specs=pl.BlockSpec((tm, D), lambda i: (i, 0)),
        out_shape=jax.ShapeDtypeStruct((T, D), F32),
        scratch_shapes=[
            pltpu.VMEM((TOP_K, tm, D), F32),
            pltpu.SMEM((TOP_K * tm,), jnp.int32),
            pltpu.SemaphoreType.DMA,
            pltpu.SemaphoreType.DMA,
        ],
        compiler_params=_cparams(("arbitrary",)),
        name="combine",
    )(x1, wts, p2, g, w_gate, w_proj, pos_tiles, y_sorted)


def _routing_tables(route, counts, n_tokens):
    te = EXPERT_TILE
    tm = TOK_TILE
    idx = route[:TOP_K]
    rank = route[TOP_K:]
    cnt = counts[:, 0]
    padded = ((cnt + te - 1) // te) * te
    ends = jnp.cumsum(padded)
    offs = ends - padded
    pos = offs[idx] + rank
    pos_tiles = pos.reshape(TOP_K, n_tokens // tm, tm).transpose(1, 0, 2).reshape(n_tokens // tm, TOP_K * tm)
    n_rows = n_tokens * TOP_K + N_EXPERTS * te
    starts = jnp.arange(n_rows // te, dtype=jnp.int32) * te
    tile_expert = jnp.minimum(jnp.searchsorted(ends, starts, side="right"), N_EXPERTS - 1).astype(jnp.int32)
    n_used = (ends[-1] // te).astype(jnp.int32).reshape(1)
    return pos_tiles.astype(jnp.int32), tile_expert, n_used, n_rows


def kernel(x, p, mix_norm_g, w_in, conv_w, conv_b, igate_b, fgate_b, mlstm_head_g, q_norm_g, k_norm_g, w_out, ffn_norm_g, w_router, b_router, w_gate_up, b_gate_up, w_down, b_down, ple_norm_g, w_ple_gate, w_ple_proj):
    B, S, D = x.shape
    T = B * S
    depth = mix_norm_g.shape[0]
    for l in range(depth):
        h_m, h_b = _mixing(x, mix_norm_g[l], w_in[l], conv_w[l], conv_b[l], igate_b[l], fgate_b[l],
                           mlstm_head_g[l], q_norm_g[l], k_norm_g[l])
        b_r = jnp.broadcast_to(b_router[l].reshape(N_EXPERTS, 1), (N_EXPERTS, LANES))
        x1, hn, route, wts, counts = _out_proj(
            x.reshape(T, D), h_m.reshape(T, GROUP_WIDTH), h_b.reshape(T, GROUP_WIDTH),
            w_out[l].astype(BF16), ffn_norm_g[l].reshape(1, D), w_router[l].T, b_r)
        pos_tiles, tile_expert, n_used, n_rows = _routing_tables(route, counts, T)
        x_sorted = _dispatch(hn, pos_tiles, jnp.zeros((n_rows, D), F32))
        y_sorted = _experts(tile_expert, n_used, x_sorted, w_gate_up[l], b_gate_up[l], w_down[l], b_down[l])
        out = _combine(x1, wts, p[l].reshape(T, -1), ple_norm_g[l].reshape(1, D),
                       w_ple_gate[l].astype(BF16), w_ple_proj[l].astype(BF16), pos_tiles, y_sorted)
        x = out.reshape(B, S, D)
    return x
```

```python
import functools

import jax
import jax.numpy as jnp
from jax import lax
from jax.experimental import pallas as pl
from jax.experimental.pallas import tpu as pltpu

F32 = jnp.float32
BF16 = jnp.bfloat16
HIGHEST = lax.Precision.HIGHEST

HEAD_DIM = 64
LANES = 128
N_HEADS = 8
GROUP_WIDTH = N_HEADS * HEAD_DIM
N_PAIRS = N_HEADS // 2
MLSTM_CHUNK = 128
CONV_WIDTH = 4
MOBA_BLOCK = 256
MOBA_TOPK = 3
N_EXPERTS = 32
TOP_K = 4
SWIGLU_LIMIT = 7.0
SWIGLU_ALPHA = 1.702
RMS_EPS = 1e-6
NEG = -1e30
VMEM_LIMIT = 56 * 1024 * 1024

COL_QK = 0
COL_V = 2 * GROUP_WIDTH
COL_O = 3 * GROUP_WIDTH
COL_MQ = 4 * GROUP_WIDTH
COL_MK = 5 * GROUP_WIDTH
COL_MV = 6 * GROUP_WIDTH
MAIN_COLS = 7 * GROUP_WIDTH


def _cparams(sem):
    return pltpu.CompilerParams(dimension_semantics=sem, vmem_limit_bytes=VMEM_LIMIT)


def _dot(a, b, **kw):
    return jnp.dot(a, b, preferred_element_type=F32, **kw)


def _dot_nt(a, b, **kw):
    return lax.dot_general(a, b, (((1,), (1,)), ((), ())), preferred_element_type=F32, **kw)


def _sigmoid(x):
    return 1.0 / (1.0 + jnp.exp(-x))


def _lane_half_mask(shape):
    return lax.broadcasted_iota(jnp.int32, shape, len(shape) - 1) < HEAD_DIM


def _pair_block_ones():
    r = lax.broadcasted_iota(jnp.int32, (LANES, LANES), 0) // HEAD_DIM
    c = lax.broadcasted_iota(jnp.int32, (LANES, LANES), 1) // HEAD_DIM
    return r == c


def _in_proj_kernel(x_ref, g_ref, w_ref, wg_ref, u_ref, gates_ref, *, n_chunk):
    x = x_ref[...]
    h = x * lax.rsqrt(jnp.mean(x * x, axis=-1, keepdims=True) + RMS_EPS) * g_ref[...]
    hb = h.astype(BF16)
    for c in range(MAIN_COLS // n_chunk):
        u_ref[:, c * n_chunk:(c + 1) * n_chunk] = _dot(hb, w_ref[:, c * n_chunk:(c + 1) * n_chunk])
    gates_ref[...] = _dot(hb, wg_ref[...])


def _in_proj(x2, g, w_main, w_gates, tm=512):
    T, D = x2.shape
    return pl.pallas_call(
        functools.partial(_in_proj_kernel, n_chunk=GROUP_WIDTH),
        grid=(T // tm,),
        in_specs=[
            pl.BlockSpec((tm, D), lambda i: (i, 0)),
            pl.BlockSpec((1, D), lambda i: (0, 0)),
            pl.BlockSpec((D, MAIN_COLS), lambda i: (0, 0)),
            pl.BlockSpec((D, LANES), lambda i: (0, 0)),
        ],
        out_specs=[
            pl.BlockSpec((tm, MAIN_COLS), lambda i: (i, 0)),
            pl.BlockSpec((tm, LANES), lambda i: (i, 0)),
        ],
        out_shape=[
            jax.ShapeDtypeStruct((T, MAIN_COLS), F32),
            jax.ShapeDtypeStruct((T, LANES), F32),
        ],
        compiler_params=_cparams(("parallel",)),
        name="in_proj",
    )(x2, g, w_main, w_gates)


def _mlstm_kernel(qk_ref, v_ref, o_ref, gates_ref, cw_ref, cb_ref, gb_ref, hg_ref,
                  out_ref, ext_ref, c_state, n_state, m_state):
    L = MLSTM_CHUNK
    W = GROUP_WIDTH
    chunk = pl.program_id(1)

    @pl.when(chunk == 0)
    def _():
        ext_ref[0:8, :] = jnp.zeros((8, 2 * W), F32)
        c_state[...] = jnp.zeros_like(c_state)
        n_state[...] = jnp.zeros_like(n_state)
        m_state[...] = jnp.zeros_like(m_state)

    ext_ref[8:8 + L, :] = qk_ref[0]
    y = cb_ref[...] + ext_ref[8:8 + L, :] * cw_ref[3:4, :]
    for j in range(1, CONV_WIDTH):
        y = y + ext_ref[8 - j:8 - j + L, :] * cw_ref[3 - j:4 - j, :]
    ext_ref[0:8, :] = ext_ref[L:L + 8, :]
    a = y * _sigmoid(y)
    q_all = a[:, :W] * (HEAD_DIM ** -0.5)
    k_all = a[:, W:]

    pre = gates_ref[0] + gb_ref[...]
    lane = lax.broadcasted_iota(jnp.int32, (L, LANES), 1)
    logsig = jnp.minimum(pre, 0.0) - jnp.log(1.0 + jnp.exp(-jnp.abs(pre)))
    G = jnp.where(lane < N_HEADS, pre, logsig)
    row = lax.broadcasted_iota(jnp.int32, (L, L), 0)
    col = lax.broadcasted_iota(jnp.int32, (L, L), 1)
    causal = col <= row
    Bc = _dot(causal.astype(F32), G, precision=HIGHEST)
    GT = G.T
    BT = Bc.T

    first_half = _lane_half_mask((L, LANES))
    first_half_row = _lane_half_mask((1, LANES))
    blockdiag = _pair_block_ones()
    ones_blk = blockdiag.astype(F32)

    for p in range(N_PAIRS):
        sl = slice(p * LANES, (p + 1) * LANES)
        q_pair = q_all[:, sl]
        k_pair = k_all[:, sl]
        v_pair = v_ref[0][:, sl]
        qb = q_pair.astype(BF16)
        kb = k_pair.astype(BF16)
        vb = v_pair.astype(BF16)
        Cp = c_state[p]
        Np = n_state[p]
        q_c = _dot(qb, Cp.astype(BF16))
        q_n = _dot(q_pair, Np, precision=HIGHEST)
        h_halves = []
        w_cols = []
        decays = []
        for hh in range(2):
            h = 2 * p + hh
            bcol = Bc[:, N_HEADS + h:N_HEADS + h + 1]
            brow = BT[N_HEADS + h:N_HEADS + h + 1, :]
            irow = GT[h:h + 1, :]
            icol = G[:, h:h + 1]
            m_prev = m_state[h:h + 1, 0:1]
            logD = jnp.where(causal, bcol - brow + irow, -jnp.inf)
            inter = bcol + m_prev
            m_t = jnp.maximum(inter, jnp.max(logD, axis=-1, keepdims=True))
            Dm = jnp.exp(logD - m_t)
            sc = jnp.exp(inter - m_t)
            in_head = first_half if hh == 0 else jnp.logical_not(first_half)
            q_h = jnp.where(in_head, q_pair, 0.0).astype(BF16)
            s = _dot_nt(q_h, kb) * Dm
            num = _dot(s.astype(BF16), vb) + sc * q_c
            den = jnp.sum(s, axis=-1, keepdims=True) + sc * q_n
            h_halves.append(num / jnp.maximum(jnp.abs(den), jnp.exp(-m_t)))
            bL = bcol[L - 1:L, :]
            logw = bL - bcol + icol
            m_new = jnp.maximum(bL + m_prev, jnp.max(logw, axis=0, keepdims=True))
            w_cols.append(jnp.exp(logw - m_new))
            decays.append(jnp.exp(bL + m_prev - m_new))
            m_state[h:h + 1, :] = jnp.broadcast_to(m_new, (1, LANES))
        h_pair = jnp.where(first_half, h_halves[0], h_halves[1])
        wk = jnp.where(first_half, w_cols[0], w_cols[1]) * k_pair
        wkT = wk.T
        dec = jnp.where(first_half_row, decays[0], decays[1])
        c_state[p] = dec * Cp + jnp.where(blockdiag, _dot(wkT.astype(BF16), vb), 0.0)
        n_state[p] = dec * Np + jnp.where(blockdiag, jnp.sum(wkT, axis=-1, keepdims=True), 0.0)
        ms = _dot(h_pair * h_pair, ones_blk, precision=HIGHEST) * (1.0 / HEAD_DIM)
        hn = h_pair * lax.rsqrt(ms + RMS_EPS) * hg_ref[:, sl]
        out_ref[0, :, sl] = (hn * _sigmoid(o_ref[0][:, sl])).astype(out_ref.dtype)


def _mlstm(u3, gates3, conv_w, conv_b, gate_b, head_g):
    B, S, _ = u3.shape
    L = MLSTM_CHUNK
    W = GROUP_WIDTH
    return pl.pallas_call(
        _mlstm_kernel,
        grid=(B, S // L),
        in_specs=[
            pl.BlockSpec((1, L, 2 * W), lambda b, c: (b, c, COL_QK // (2 * W))),
            pl.BlockSpec((1, L, W), lambda b, c: (b, c, COL_V // W)),
            pl.BlockSpec((1, L, W), lambda b, c: (b, c, COL_O // W)),
            pl.BlockSpec((1, L, LANES), lambda b, c: (b, c, 0)),
            pl.BlockSpec((CONV_WIDTH, 2 * W), lambda b, c: (0, 0)),
            pl.BlockSpec((1, 2 * W), lambda b, c: (0, 0)),
            pl.BlockSpec((1, LANES), lambda b, c: (0, 0)),
            pl.BlockSpec((1, W), lambda b, c: (0, 0)),
        ],
        out_specs=pl.BlockSpec((1, L, W), lambda b, c: (b, c, 0)),
        out_shape=jax.ShapeDtypeStruct((B, S, W), BF16),
        scratch_shapes=[
            pltpu.VMEM((L + 8, 2 * W), F32),
            pltpu.VMEM((N_PAIRS, LANES, LANES), F32),
            pltpu.VMEM((N_PAIRS, LANES, LANES), F32),
            pltpu.VMEM((N_HEADS, LANES), F32),
        ],
        compiler_params=_cparams(("parallel", "arbitrary")),
        name="mlstm",
    )(u3, u3, u3, gates3, conv_w, conv_b, gate_b, head_g)


AUX_GATE = 8


def _moba_kernel(q_ref, k_ref, v_ref, qg_ref, kg_ref, out_ref, kaug, vaug_t, kmean, s_buf, *, n_blocks):
    BS = MOBA_BLOCK
    HD = HEAD_DIM
    pair = pl.program_id(1)
    own = pl.program_id(2)

    @pl.when(own == 0)
    def _():
        kmean[...] = jnp.zeros_like(kmean)

    lane = lax.broadcasted_iota(jnp.int32, (BS, LANES), 1)
    rowi = lax.broadcasted_iota(jnp.int32, (BS, LANES), 0)
    ones_blk = _pair_block_ones().astype(F32)

    k_pair = k_ref[0]
    k_ms = _dot(k_pair * k_pair, ones_blk, precision=HIGHEST) * (1.0 / HD)
    kn = k_pair * lax.rsqrt(k_ms + RMS_EPS) * kg_ref[...]
    kmean[pl.ds(own, 1), :] = jnp.mean(kn, axis=0, keepdims=True)

    q_t = q_ref[0].T
    v_t = v_ref[0].T
    row_t = lax.broadcasted_iota(jnp.int32, (LANES, BS), 0)
    blk = lax.broadcasted_iota(jnp.int32, (n_blocks, BS), 0)
    blk_f = blk.astype(F32)
    km = kmean[...]
    lane_km = lax.broadcasted_iota(jnp.int32, (n_blocks, LANES), 1)

    q_aug_t, k_blk, v_blk_t = [], [], []
    for hh in range(2):
        head = 2 * pair + hh
        slope = jnp.exp2(-jnp.full((1, 1), 1.0, F32) * (head + 1).astype(F32))
        base = HD if hh == 0 else 0
        hs = slice(hh * HD, (hh + 1) * HD)
        q_h = q_t[hs, :]
        q_ms = jnp.mean(q_h * q_h, axis=0, keepdims=True)
        qn_h = q_h * lax.rsqrt(q_ms + RMS_EPS) * qg_ref[hs, :] * (HD ** -0.5)

        km_h = jnp.where((lane_km >= hh * HD) & (lane_km < (hh + 1) * HD), km, 0.0)
        zeros_h = jnp.zeros((HD, BS), F32)
        qn_full = jnp.concatenate([qn_h, zeros_h] if hh == 0 else [zeros_h, qn_h], axis=0)
        gate = _dot(km_h, qn_full, precision=HIGHEST)
        g = jnp.where(blk < own, gate, -jnp.inf)
        sel = blk == own
        for _ in range(MOBA_TOPK):
            mx = jnp.max(g, axis=0, keepdims=True)
            first = jnp.min(jnp.where(g == mx, blk_f, 1e9), axis=0, keepdims=True)
            pick = (blk_f == first) & (mx > -jnp.inf)
            sel = sel | pick
            g = jnp.where(pick, -jnp.inf, g)
        bias = jnp.where(sel, 0.0, NEG)
        r8 = lax.broadcasted_iota(jnp.int32, (AUX_GATE, BS), 0)
        ones2 = jnp.where(r8 < 2, 1.0, 0.0)
        aux = jnp.concatenate([ones2, bias, jnp.zeros((HD - AUX_GATE - n_blocks, BS), F32)], axis=0)
        parts = [qn_h, aux] if hh == 0 else [aux, qn_h]
        q_aug_t.append(jnp.concatenate(parts, axis=0).astype(BF16))

        in_head = (lane >= hh * HD) & (lane < (hh + 1) * HD)
        k_aux = jnp.where(lane == base, slope * rowi.astype(F32),
                          jnp.where(lane == base + 1, slope * (own * BS).astype(F32),
                                    jnp.where(lane - (base + AUX_GATE) == own, 1.0, 0.0)))
        k_own = jnp.where(in_head, kn, k_aux).astype(BF16)
        in_head_t = (row_t >= hh * HD) & (row_t < (hh + 1) * HD)
        v_own_t = jnp.where(in_head_t, v_t, jnp.where(row_t == base, 1.0, 0.0)).astype(BF16)
        kaug[hh, own] = k_own
        vaug_t[hh, own] = v_own_t
        k_blk.append(k_own)
        v_blk_t.append(v_own_t)

    r2 = lax.broadcasted_iota(jnp.int32, (BS, BS), 0)
    c2 = lax.broadcasted_iota(jnp.int32, (BS, BS), 1)
    causal_t = r2 <= c2
    carry = []
    for hh in range(2):
        s = jnp.where(causal_t, _dot(k_blk[hh], q_aug_t[hh]), NEG)
        m = jnp.max(s, axis=0, keepdims=True)
        p = jnp.exp(s - m)
        carry += [m, _dot(v_blk_t[hh], p.astype(BF16))]

    def scores(hh, j):
        return _dot(kaug[hh, j], q_aug_t[hh])

    def update(hh, j, s, m, acc):
        m_new = jnp.maximum(m, jnp.max(s, axis=0, keepdims=True))
        p = jnp.exp(s - m_new)
        return m_new, jnp.exp(m - m_new) * acc + _dot(vaug_t[hh, j], p.astype(BF16))

    def score_pair(slot, t):
        for u in range(2):
            for hh in range(2):
                s_buf[slot, u, hh] = scores(hh, 2 * t + u)

    half = own // 2

    @pl.when(half > 0)
    def _():
        score_pair(0, 0)

    def body2(t, carry):
        carry = list(carry)
        slot = lax.rem(t, 2)
        for u in range(2):
            for hh in range(2):
                carry[2 * hh], carry[2 * hh + 1] = update(
                    hh, 2 * t + u, s_buf[slot, u, hh], carry[2 * hh], carry[2 * hh + 1])
        score_pair(1 - slot, jnp.minimum(t + 1, half - 1))
        return tuple(carry)

    def body1(j, carry):
        carry = list(carry)
        for hh in range(2):
            carry[2 * hh], carry[2 * hh + 1] = update(hh, j, scores(hh, j), carry[2 * hh], carry[2 * hh + 1])
        return tuple(carry)

    carry = lax.fori_loop(0, half, body2, tuple(carry))
    carry = lax.fori_loop(2 * half, own, body1, carry)

    outs = []
    for hh in range(2):
        base = HD if hh == 0 else 0
        acc = carry[2 * hh + 1]
        outs.append(acc / acc[base:base + 1, :])
    out_t = jnp.where(row_t < HD, outs[0], outs[1])
    out_ref[0] = out_t.T.astype(out_ref.dtype)


def _moba(u3, q_g_t, k_g):
    B, S, _ = u3.shape
    BS = MOBA_BLOCK
    nb = S // BS
    assert AUX_GATE + nb <= HEAD_DIM and nb % 8 == 0
    return pl.pallas_call(
        functools.partial(_moba_kernel, n_blocks=nb),
        grid=(B, N_PAIRS, nb),
        in_specs=[
            pl.BlockSpec((1, BS, LANES), lambda b, p, i: (b, i, COL_MQ // LANES + p)),
            pl.BlockSpec((1, BS, LANES), lambda b, p, i: (b, i, COL_MK // LANES + p)),
            pl.BlockSpec((1, BS, LANES), lambda b, p, i: (b, i, COL_MV // LANES + p)),
            pl.BlockSpec((LANES, BS), lambda b, p, i: (0, 0)),
            pl.BlockSpec((1, LANES), lambda b, p, i: (0, 0)),
        ],
        out_specs=pl.BlockSpec((1, BS, LANES), lambda b, p, i: (b, i, p)),
        out_shape=jax.ShapeDtypeStruct((B, S, GROUP_WIDTH), BF16),
        scratch_shapes=[
            pltpu.VMEM((2, nb, BS, LANES), BF16),
            pltpu.VMEM((2, nb, LANES, BS), BF16),
            pltpu.VMEM((nb, LANES), F32),
            pltpu.VMEM((2, 2, 2, BS, BS), F32),
        ],
        compiler_params=_cparams(("parallel", "parallel", "arbitrary")),
        name="moba",
    )(u3, u3, u3, q_g_t, k_g)


def _mixing(x, mix_norm_g, w_in, conv_w, conv_b, igate_b, fgate_b, mlstm_head_g, q_norm_g, k_norm_g):
    B, S, D = x.shape
    W = GROUP_WIDTH
    n_gate = 2 * N_HEADS
    w_main = jnp.concatenate([w_in[:, :4 * W], w_in[:, 4 * W + n_gate:]], axis=1).astype(BF16)
    w_gates = jnp.pad(w_in[:, 4 * W:4 * W + n_gate], ((0, 0), (0, LANES - n_gate))).astype(BF16)
    u, gates = _in_proj(x.reshape(B * S, D), mix_norm_g.reshape(1, D), w_main, w_gates)
    u3 = u.reshape(B, S, MAIN_COLS)
    gates3 = gates.reshape(B, S, LANES)
    gate_b = jnp.pad(jnp.concatenate([igate_b, fgate_b]), (0, LANES - n_gate)).reshape(1, LANES)
    h_m = _mlstm(u3, gates3, conv_w, conv_b.reshape(1, 2 * W), gate_b, mlstm_head_g.reshape(1, W))
    qg_t = jnp.broadcast_to(jnp.concatenate([q_norm_g, q_norm_g]).reshape(LANES, 1), (LANES, MOBA_BLOCK))
    kg2 = jnp.concatenate([k_norm_g, k_norm_g]).reshape(1, LANES)
    h_b = _moba(u3, qg_t, kg2)
    return h_m, h_b


TOK_TILE = 256
EXPERT_TILE = 256


def _out_proj_kernel(x_ref, hm_ref, hb_ref, wo_ref, g_ref, wr_ref, br_ref,
                     x1_ref, hn_ref, route_ref, wts_ref, counts_ref, carry):
    tm = TOK_TILE
    W = GROUP_WIDTH
    step = pl.program_id(0)

    @pl.when(step == 0)
    def _():
        carry[...] = jnp.zeros_like(carry)

    x1 = x_ref[...] + _dot(hm_ref[...], wo_ref[0:W, :]) + _dot(hb_ref[...], wo_ref[W:2 * W, :])
    x1_ref[...] = x1
    hn = x1 * lax.rsqrt(jnp.mean(x1 * x1, axis=-1, keepdims=True) + RMS_EPS) * g_ref[...]
    hn_ref[...] = hn

    logits = _dot_nt(wr_ref[...], hn, precision=HIGHEST) + br_ref[:, 0:1]
    eidx = lax.broadcasted_iota(jnp.int32, (N_EXPERTS, tm), 0).astype(F32)
    vals, onehots, idxs = [], [], []
    l = logits
    for _ in range(TOP_K):
        mx = jnp.max(l, axis=0, keepdims=True)
        first = jnp.min(jnp.where(l == mx, eidx, 1e9), axis=0, keepdims=True)
        pick = eidx == first
        vals.append(mx)
        idxs.append(first)
        onehots.append(pick.astype(F32))
        l = jnp.where(pick, -jnp.inf, l)
    exps = [jnp.exp(v - vals[0]) for v in vals]
    inv = 1.0 / (exps[0] + exps[1] + exps[2] + exps[3])
    total = onehots[0] + onehots[1] + onehots[2] + onehots[3]
    r2 = lax.broadcasted_iota(jnp.int32, (tm, tm), 0)
    c2 = lax.broadcasted_iota(jnp.int32, (tm, tm), 1)
    incl = (r2 <= c2).astype(BF16)
    cum = _dot(total.astype(BF16), incl)
    excl = cum - total + carry[:, 0:1]
    ranks = [jnp.sum(oh * excl, axis=0, keepdims=True) for oh in onehots]
    carry[...] = carry[...] + jnp.sum(total, axis=-1, keepdims=True)
    counts_ref[...] = carry[...].astype(jnp.int32)

    row8 = lax.broadcasted_iota(jnp.int32, (8, tm), 0)
    route = jnp.zeros((8, tm), F32)
    for k in range(TOP_K):
        route = jnp.where(row8 == k, idxs[k], route)
        route = jnp.where(row8 == TOP_K + k, ranks[k], route)
    route_ref[...] = route.astype(jnp.int32)
    rowl = lax.broadcasted_iota(jnp.int32, (LANES, tm), 0)
    wt = jnp.zeros((LANES, tm), F32)
    for k in range(TOP_K):
        wt = jnp.where(rowl == k, exps[k] * inv, wt)
    wts_ref[...] = wt.T


def _out_proj(x2, h_m, h_b, w_out, g, w_router_t, b_router):
    T, D = x2.shape
    tm = TOK_TILE
    W = GROUP_WIDTH
    return pl.pallas_call(
        _out_proj_kernel,
        grid=(T // tm,),
        in_specs=[
            pl.BlockSpec((tm, D), lambda i: (i, 0)),
            pl.BlockSpec((tm, W), lambda i: (i, 0)),
            pl.BlockSpec((tm, W), lambda i: (i, 0)),
            pl.BlockSpec((2 * W, D), lambda i: (0, 0)),
            pl.BlockSpec((1, D), lambda i: (0, 0)),
            pl.BlockSpec((N_EXPERTS, D), lambda i: (0, 0)),
            pl.BlockSpec((N_EXPERTS, LANES), lambda i: (0, 0)),
        ],
        out_specs=[
            pl.BlockSpec((tm, D), lambda i: (i, 0)),
            pl.BlockSpec((tm, D), lambda i: (i, 0)),
            pl.BlockSpec((8, tm), lambda i: (0, i)),
            pl.BlockSpec((tm, LANES), lambda i: (i, 0)),
            pl.BlockSpec((N_EXPERTS, LANES), lambda i: (0, 0)),
        ],
        out_shape=[
            jax.ShapeDtypeStruct((T, D), F32),
            jax.ShapeDtypeStruct((T, D), F32),
            jax.ShapeDtypeStruct((8, T), jnp.int32),
            jax.ShapeDtypeStruct((T, LANES), F32),
            jax.ShapeDtypeStruct((N_EXPERTS, LANES), jnp.int32),
        ],
        scratch_shapes=[pltpu.VMEM((N_EXPERTS, LANES), F32)],
        compiler_params=_cparams(("arbitrary",)),
        name="out_proj",
    )(x2, h_m, h_b, w_out, g, w_router_t, b_router)


def _row_copy(src, s_row, dst, d_row, sem):
    return pltpu.make_async_copy(src.at[pl.ds(s_row, 1), :], dst.at[pl.ds(d_row, 1), :], sem)


def _dispatch_kernel(ends_ref, padded_ref, hn_ref, pos_hbm, xs_hbm, zbuf, pos_smem, pos_sem, row_sem, z_sem):
    tm = TOK_TILE
    te = EXPERT_TILE
    step = pl.program_id(0)

    @pl.when(step == 0)
    def _():
        zbuf[...] = jnp.zeros_like(zbuf)
        total = ends_ref[N_EXPERTS - 1]
        n_rows = xs_hbm.shape[0]

        def zero_tile(start):
            return pltpu.make_async_copy(zbuf, xs_hbm.at[pl.ds(pl.multiple_of(start, te), te), :], z_sem)

        for e in range(N_EXPERTS):
            @pl.when(padded_ref[e] > 0)
            def _():
                zero_tile(ends_ref[e] - te).start()

            @pl.when(total + e * te < n_rows)
            def _():
                zero_tile(total + e * te).start()
        for e in range(N_EXPERTS):
            @pl.when(padded_ref[e] > 0)
            def _():
                zero_tile(0).wait()

            @pl.when(total + e * te < n_rows)
            def _():
                zero_tile(0).wait()

    cp = pltpu.make_async_copy(pos_hbm.at[step], pos_smem, pos_sem)
    cp.start()
    cp.wait()

    def issue(t, _):
        for k in range(TOP_K):
            _row_copy(hn_ref, t, xs_hbm, pos_smem[k * tm + t], row_sem).start(priority=k % 2)
        return 0

    lax.fori_loop(0, tm, issue, 0)

    def drain(t, _):
        for k in range(TOP_K):
            _row_copy(hn_ref, 0, xs_hbm, 0, row_sem).wait()
        return 0

    lax.fori_loop(0, tm, drain, 0)


def _dispatch(hn, pos_tiles, ends, padded, n_rows):
    T, D = hn.shape
    tm = TOK_TILE
    grid_spec = pltpu.PrefetchScalarGridSpec(
        num_scalar_prefetch=2,
        grid=(T // tm,),
        in_specs=[
            pl.BlockSpec((tm, D), lambda i, e, p: (i, 0)),
            pl.BlockSpec(memory_space=pl.ANY),
        ],
        out_specs=pl.BlockSpec(memory_space=pl.ANY),
        scratch_shapes=[
            pltpu.VMEM((EXPERT_TILE, D), F32),
            pltpu.SMEM((TOP_K * tm,), jnp.int32),
            pltpu.SemaphoreType.DMA,
            pltpu.SemaphoreType.DMA,
            pltpu.SemaphoreType.DMA,
        ],
    )
    return pl.pallas_call(
        _dispatch_kernel,
        grid_spec=grid_spec,
        out_shape=jax.ShapeDtypeStruct((n_rows, D), F32),
        compiler_params=_cparams(("arbitrary",)),
        name="dispatch",
    )(ends, padded, hn, pos_tiles)


def _experts_kernel(te_ref, nu_ref, x_ref, wgu_ref, bgu_ref, wdn_ref, bdn_ref, y_ref, wgu_b, wdn_b):
    i = pl.program_id(0)
    F = wdn_ref.shape[1]
    prev = te_ref[jnp.maximum(i - 1, 0)]
    fresh = (i == 0) | (te_ref[i] != prev)

    @pl.when(fresh & (i < nu_ref[0]))
    def _():
        wgu_b[...] = wgu_ref[0].astype(BF16)
        wdn_b[...] = wdn_ref[0].astype(BF16)

    @pl.when(i < nu_ref[0])
    def _():
        xb = x_ref[...].astype(BF16)
        hu = _dot(xb, wgu_b[...]) + bgu_ref[0]
        glu = jnp.minimum(hu[:, :F], SWIGLU_LIMIT)
        lin = jnp.clip(hu[:, F:], -SWIGLU_LIMIT, SWIGLU_LIMIT)
        act = glu * _sigmoid(SWIGLU_ALPHA * glu) * (lin + 1.0)
        y_ref[...] = _dot(act.astype(BF16), wdn_b[...]) + bdn_ref[0]

    @pl.when(i >= nu_ref[0])
    def _():
        y_ref[...] = jnp.zeros_like(y_ref)


def _experts(tile_expert, n_used, x_sorted, w_gu, b_gu, w_dn, b_dn):
    P, D = x_sorted.shape
    E, _, F2 = w_gu.shape
    F = F2 // 2
    tm = EXPERT_TILE
    n_tiles = P // tm

    def row_map(i, te, nu):
        return (jnp.minimum(i, nu[0] - 1), 0)

    def exp_map(i, te, nu):
        return (te[i], 0, 0)

    grid_spec = pltpu.PrefetchScalarGridSpec(
        num_scalar_prefetch=2,
        grid=(n_tiles,),
        in_specs=[
            pl.BlockSpec((tm, D), row_map),
            pl.BlockSpec((1, D, F2), exp_map),
            pl.BlockSpec((1, 1, F2), exp_map),
            pl.BlockSpec((1, F, D), exp_map),
            pl.BlockSpec((1, 1, D), exp_map),
        ],
        out_specs=pl.BlockSpec((tm, D), lambda i, te, nu: (i, 0)),
        scratch_shapes=[pltpu.VMEM((D, F2), BF16), pltpu.VMEM((F, D), BF16)],
    )
    return pl.pallas_call(
        _experts_kernel,
        grid_spec=grid_spec,
        out_shape=jax.ShapeDtypeStruct((P, D), F32),
        compiler_params=_cparams(("arbitrary",)),
        name="experts",
    )(tile_expert, n_used, x_sorted, w_gu, b_gu.reshape(E, 1, F2), w_dn, b_dn.reshape(E, 1, D))


def _combine_kernel(x1_ref, wts_ref, p_ref, g_ref, wg_ref, wp_ref, pos_hbm, ys_hbm,
                    out_ref, ybuf, pos_smem, pos_sem, row_sem):
    tm = TOK_TILE
    step = pl.program_id(0)
    cp = pltpu.make_async_copy(pos_hbm.at[step], pos_smem, pos_sem)
    cp.start()
    cp.wait()

    def issue(t, _):
        for k in range(TOP_K):
            _row_copy(ys_hbm, pos_smem[k * tm + t], ybuf.at[k], t, row_sem).start(priority=k % 2)
        return 0

    lax.fori_loop(0, tm, issue, 0)

    def drain(t, _):
        for k in range(TOP_K):
            _row_copy(ys_hbm, 0, ybuf.at[k], 0, row_sem).wait()
        return 0

    lax.fori_loop(0, tm, drain, 0)

    w = wts_ref[...]
    x2 = x1_ref[...]
    for k in range(TOP_K):
        x2 = x2 + w[:, k:k + 1] * ybuf[k]
    hn = x2 * lax.rsqrt(jnp.mean(x2 * x2, axis=-1, keepdims=True) + RMS_EPS) * g_ref[...]
    gate = _sigmoid(_dot(hn.astype(BF16), wg_ref[...]))
    out_ref[...] = x2 + gate * _dot(p_ref[...].astype(BF16), wp_ref[...])


def _combine(x1, wts, p2, g, w_gate, w_proj, pos_tiles, y_sorted):
    T, D = x1.shape
    tm = TOK_TILE
    PD = p2.shape[1]
    return pl.pallas_call(
        _combine_kernel,
        grid=(T // tm,),
        in_specs=[
            pl.BlockSpec((tm, D), lambda i: (i, 0)),
            pl.BlockSpec((tm, LANES), lambda i: (i, 0)),
            pl.BlockSpec((tm, PD), lambda i: (i, 0)),
            pl.BlockSpec((1, D), lambda i: (0, 0)),
            pl.BlockSpec((D, D), lambda i: (0, 0)),
            pl.BlockSpec((PD, D), lambda i: (0, 0)),
            pl.BlockSpec(memory_space=pl.ANY),
            pl.BlockSpec(memory_space=pl.ANY),
        ],
        out_specs=pl.BlockSpec((tm, D), lambda i: (i, 0)),
        out_shape=jax.ShapeDtypeStruct((T, D), F32),
        scratch_shapes=[
            pltpu.VMEM((TOP_K, tm, D), F32),
            pltpu.SMEM((TOP_K * tm,), jnp.int32),
            pltpu.SemaphoreType.DMA,
            pltpu.SemaphoreType.DMA,
        ],
        compiler_params=_cparams(("arbitrary",)),
        name="combine",
    )(x1, wts, p2, g, w_gate, w_proj, pos_tiles, y_sorted)


def _routing_tables(route, counts, n_tokens):
    te = EXPERT_TILE
    tm = TOK_TILE
    idx = route[:TOP_K]
    rank = route[TOP_K:]
    cnt = counts[:, 0]
    padded = ((cnt + te - 1) // te) * te
    ends = jnp.cumsum(padded)
    offs = ends - padded
    e_ids = jnp.arange(N_EXPERTS, dtype=jnp.int32)[:, None, None]
    pos = jnp.sum(jnp.where(idx[None] == e_ids, offs[:, None, None], 0), axis=0) + rank
    pos_tiles = pos.reshape(TOP_K, n_tokens // tm, tm).transpose(1, 0, 2).reshape(n_tokens // tm, TOP_K * tm)
    n_rows = n_tokens * TOP_K + N_EXPERTS * te
    starts = jnp.arange(n_rows // te, dtype=jnp.int32) * te
    tile_expert = jnp.minimum(jnp.sum((starts[:, None] >= ends[None, :]).astype(jnp.int32), axis=1), N_EXPERTS - 1)
    n_used = (ends[-1] // te).astype(jnp.int32).reshape(1)
    return (pos_tiles.astype(jnp.int32), tile_expert.astype(jnp.int32), n_used,
            ends.astype(jnp.int32), padded.astype(jnp.int32), n_rows)


def kernel(x, p, mix_norm_g, w_in, conv_w, conv_b, igate_b, fgate_b, mlstm_head_g, q_norm_g, k_norm_g, w_out, ffn_norm_g, w_router, b_router, w_gate_up, b_gate_up, w_down, b_down, ple_norm_g, w_ple_gate, w_ple_proj):
    B, S, D = x.shape
    T = B * S
    depth = mix_norm_g.shape[0]
    for l in range(depth):
        h_m, h_b = _mixing(x, mix_norm_g[l], w_in[l], conv_w[l], conv_b[l], igate_b[l], fgate_b[l],
                           mlstm_head_g[l], q_norm_g[l], k_norm_g[l])
        b_r = jnp.broadcast_to(b_router[l].reshape(N_EXPERTS, 1), (N_EXPERTS, LANES))
        x1, hn, route, wts, counts = _out_proj(
            x.reshape(T, D), h_m.reshape(T, GROUP_WIDTH), h_b.reshape(T, GROUP_WIDTH),
            w_out[l].astype(BF16), ffn_norm_g[l].reshape(1, D), w_router[l].T, b_r)
        pos_tiles, tile_expert, n_used, ends, padded, n_rows = _routing_tables(route, counts, T)
        x_sorted = _dispatch(hn, pos_tiles, ends, padded, n_rows)
        y_sorted = _experts(tile_expert, n_used, x_sorted, w_gate_up[l], b_gate_up[l], w_down[l], b_down[l])
        out = _combine(x1, wts, p[l].reshape(T, -1), ple_norm_g[l].reshape(1, D),
                       w_ple_gate[l].astype(BF16), w_ple_proj[l].astype(BF16), pos_tiles, y_sorted)
        x = out.reshape(B, S, D)
    return x
```

```python
import functools

import jax
import jax.numpy as jnp
from jax import lax
from jax.experimental import pallas as pl
from jax.experimental.pallas import tpu as pltpu

F32 = jnp.float32
BF16 = jnp.bfloat16
HIGHEST = lax.Precision.HIGHEST

HEAD_DIM = 64
LANES = 128
N_HEADS = 8
GROUP_WIDTH = N_HEADS * HEAD_DIM
N_PAIRS = N_HEADS // 2
MLSTM_CHUNK = 128
CONV_WIDTH = 4
MOBA_BLOCK = 256
MOBA_TOPK = 3
N_EXPERTS = 32
TOP_K = 4
SWIGLU_LIMIT = 7.0
SWIGLU_ALPHA = 1.702
RMS_EPS = 1e-6
NEG = -1e30
VMEM_LIMIT = 56 * 1024 * 1024

COL_QK = 0
COL_V = 2 * GROUP_WIDTH
COL_O = 3 * GROUP_WIDTH
COL_MQ = 4 * GROUP_WIDTH
COL_MK = 5 * GROUP_WIDTH
COL_MV = 6 * GROUP_WIDTH
MAIN_COLS = 7 * GROUP_WIDTH


def _cparams(sem):
    return pltpu.CompilerParams(dimension_semantics=sem, vmem_limit_bytes=VMEM_LIMIT)


def _dot(a, b, **kw):
    return jnp.dot(a, b, preferred_element_type=F32, **kw)


def _dot_nt(a, b, **kw):
    return lax.dot_general(a, b, (((1,), (1,)), ((), ())), preferred_element_type=F32, **kw)


def _split3(x):
    hi = x.astype(BF16)
    r = x - hi.astype(F32)
    mid = r.astype(BF16)
    lo = (r - mid.astype(F32)).astype(BF16)
    return hi, mid, lo


def _dot_rhs01(a, b01):
    b = b01.astype(BF16)
    hi, mid, lo = _split3(a)
    return _dot(hi, b) + _dot(mid, b) + _dot(lo, b)


def _dot_lhs01(a01, b):
    a = a01.astype(BF16)
    hi, mid, lo = _split3(b)
    return _dot(a, hi) + _dot(a, mid) + _dot(a, lo)


def _sigmoid(x):
    return 1.0 / (1.0 + jnp.exp(-x))


def _lane_half_mask(shape):
    return lax.broadcasted_iota(jnp.int32, shape, len(shape) - 1) < HEAD_DIM


def _pair_block_ones():
    r = lax.broadcasted_iota(jnp.int32, (LANES, LANES), 0) // HEAD_DIM
    c = lax.broadcasted_iota(jnp.int32, (LANES, LANES), 1) // HEAD_DIM
    return r == c


def _in_proj_kernel(x_ref, g_ref, w_ref, wg_ref, u_ref, gates_ref, *, n_chunk):
    x = x_ref[...]
    h = x * lax.rsqrt(jnp.mean(x * x, axis=-1, keepdims=True) + RMS_EPS) * g_ref[...]
    hb = h.astype(BF16)
    for c in range(MAIN_COLS // n_chunk):
        u_ref[:, c * n_chunk:(c + 1) * n_chunk] = _dot(hb, w_ref[:, c * n_chunk:(c + 1) * n_chunk])
    gates_ref[...] = _dot(hb, wg_ref[...])


def _in_proj(x2, g, w_main, w_gates, tm=512):
    T, D = x2.shape
    return pl.pallas_call(
        functools.partial(_in_proj_kernel, n_chunk=GROUP_WIDTH),
        grid=(T // tm,),
        in_specs=[
            pl.BlockSpec((tm, D), lambda i: (i, 0)),
            pl.BlockSpec((1, D), lambda i: (0, 0)),
            pl.BlockSpec((D, MAIN_COLS), lambda i: (0, 0)),
            pl.BlockSpec((D, LANES), lambda i: (0, 0)),
        ],
        out_specs=[
            pl.BlockSpec((tm, MAIN_COLS), lambda i: (i, 0)),
            pl.BlockSpec((tm, LANES), lambda i: (i, 0)),
        ],
        out_shape=[
            jax.ShapeDtypeStruct((T, MAIN_COLS), F32),
            jax.ShapeDtypeStruct((T, LANES), F32),
        ],
        compiler_params=_cparams(("parallel",)),
        name="in_proj",
    )(x2, g, w_main, w_gates)


def _mlstm_kernel(qk_ref, v_ref, o_ref, gates_ref, cw_ref, cb_ref, gb_ref, hg_ref,
                  out_ref, ext_ref, c_state, n_state, m_state):
    L = MLSTM_CHUNK
    W = GROUP_WIDTH
    chunk = pl.program_id(1)

    @pl.when(chunk == 0)
    def _():
        ext_ref[0:8, :] = jnp.zeros((8, 2 * W), F32)
        c_state[...] = jnp.zeros_like(c_state)
        n_state[...] = jnp.zeros_like(n_state)
        m_state[...] = jnp.zeros_like(m_state)

    ext_ref[8:8 + L, :] = qk_ref[0]
    y = cb_ref[...] + ext_ref[8:8 + L, :] * cw_ref[3:4, :]
    for j in range(1, CONV_WIDTH):
        y = y + ext_ref[8 - j:8 - j + L, :] * cw_ref[3 - j:4 - j, :]
    ext_ref[0:8, :] = ext_ref[L:L + 8, :]
    a = y * _sigmoid(y)
    q_all = a[:, :W] * (HEAD_DIM ** -0.5)
    k_all = a[:, W:]

    pre = gates_ref[0] + gb_ref[...]
    lane = lax.broadcasted_iota(jnp.int32, (L, LANES), 1)
    logsig = jnp.minimum(pre, 0.0) - jnp.log(1.0 + jnp.exp(-jnp.abs(pre)))
    G = jnp.where(lane < N_HEADS, pre, logsig)
    row = lax.broadcasted_iota(jnp.int32, (L, L), 0)
    col = lax.broadcasted_iota(jnp.int32, (L, L), 1)
    causal = col <= row
    Bc = _dot_lhs01(causal, G)
    GT = G.T
    BT = Bc.T

    first_half = _lane_half_mask((L, LANES))
    first_half_row = _lane_half_mask((1, LANES))
    blockdiag = _pair_block_ones()

    for p in range(N_PAIRS):
        sl = slice(p * LANES, (p + 1) * LANES)
        q_pair = q_all[:, sl]
        k_pair = k_all[:, sl]
        v_pair = v_ref[0][:, sl]
        qb = q_pair.astype(BF16)
        kb = k_pair.astype(BF16)
        vb = v_pair.astype(BF16)
        Cp = c_state[p]
        Np = n_state[p]
        q_c = _dot(qb, Cp.astype(BF16))
        q_n = _dot(qb, Np.astype(BF16))
        h_halves = []
        w_cols = []
        decays = []
        for hh in range(2):
            h = 2 * p + hh
            bcol = Bc[:, N_HEADS + h:N_HEADS + h + 1]
            brow = BT[N_HEADS + h:N_HEADS + h + 1, :]
            irow = GT[h:h + 1, :]
            icol = G[:, h:h + 1]
            m_prev = m_state[h:h + 1, 0:1]
            logD = jnp.where(causal, bcol - brow + irow, -jnp.inf)
            inter = bcol + m_prev
            m_t = jnp.maximum(inter, jnp.max(logD, axis=-1, keepdims=True))
            Dm = jnp.exp(logD - m_t)
            sc = jnp.exp(inter - m_t)
            in_head = first_half if hh == 0 else jnp.logical_not(first_half)
            q_h = jnp.where(in_head, q_pair, 0.0).astype(BF16)
            s = _dot_nt(q_h, kb) * Dm
            num = _dot(s.astype(BF16), vb) + sc * q_c
            den = jnp.sum(s, axis=-1, keepdims=True) + sc * q_n
            h_halves.append(num / jnp.maximum(jnp.abs(den), jnp.exp(-m_t)))
            bL = bcol[L - 1:L, :]
            logw = bL - bcol + icol
            m_new = jnp.maximum(bL + m_prev, jnp.max(logw, axis=0, keepdims=True))
            w_cols.append(jnp.exp(logw - m_new))
            decays.append(jnp.exp(bL + m_prev - m_new))
            m_state[h:h + 1, :] = jnp.broadcast_to(m_new, (1, LANES))
        h_pair = jnp.where(first_half, h_halves[0], h_halves[1])
        wk = jnp.where(first_half, w_cols[0], w_cols[1]) * k_pair
        wkT = wk.T
        dec = jnp.where(first_half_row, decays[0], decays[1])
        c_state[p] = dec * Cp + jnp.where(blockdiag, _dot(wkT.astype(BF16), vb), 0.0)
        n_state[p] = dec * Np + jnp.where(blockdiag, jnp.sum(wkT, axis=-1, keepdims=True), 0.0)
        ms = _dot_rhs01(h_pair * h_pair, blockdiag) * (1.0 / HEAD_DIM)
        hn = h_pair * lax.rsqrt(ms + RMS_EPS) * hg_ref[:, sl]
        out_ref[0, :, sl] = (hn * _sigmoid(o_ref[0][:, sl])).astype(out_ref.dtype)


def _mlstm(u3, gates3, conv_w, conv_b, gate_b, head_g):
    B, S, _ = u3.shape
    L = MLSTM_CHUNK
    W = GROUP_WIDTH
    return pl.pallas_call(
        _mlstm_kernel,
        grid=(B, S // L),
        in_specs=[
            pl.BlockSpec((1, L, 2 * W), lambda b, c: (b, c, COL_QK // (2 * W))),
            pl.BlockSpec((1, L, W), lambda b, c: (b, c, COL_V // W)),
            pl.BlockSpec((1, L, W), lambda b, c: (b, c, COL_O // W)),
            pl.BlockSpec((1, L, LANES), lambda b, c: (b, c, 0)),
            pl.BlockSpec((CONV_WIDTH, 2 * W), lambda b, c: (0, 0)),
            pl.BlockSpec((1, 2 * W), lambda b, c: (0, 0)),
            pl.BlockSpec((1, LANES), lambda b, c: (0, 0)),
            pl.BlockSpec((1, W), lambda b, c: (0, 0)),
        ],
        out_specs=pl.BlockSpec((1, L, W), lambda b, c: (b, c, 0)),
        out_shape=jax.ShapeDtypeStruct((B, S, W), BF16),
        scratch_shapes=[
            pltpu.VMEM((L + 8, 2 * W), F32),
            pltpu.VMEM((N_PAIRS, LANES, LANES), F32),
            pltpu.VMEM((N_PAIRS, LANES, LANES), F32),
            pltpu.VMEM((N_HEADS, LANES), F32),
        ],
        compiler_params=_cparams(("parallel", "arbitrary")),
        name="mlstm",
    )(u3, u3, u3, gates3, conv_w, conv_b, gate_b, head_g)


AUX_GATE = 8


def _moba_kernel(q_ref, k_ref, v_ref, qg_ref, kg_ref, out_ref, kaug, vaug_t, kmean, s_buf_a, s_buf_b,
                 *, n_blocks):
    BS = MOBA_BLOCK
    HD = HEAD_DIM
    pair = pl.program_id(1)
    own = pl.program_id(2)

    @pl.when(own == 0)
    def _():
        kmean[...] = jnp.zeros_like(kmean)

    lane = lax.broadcasted_iota(jnp.int32, (BS, LANES), 1)
    rowi = lax.broadcasted_iota(jnp.int32, (BS, LANES), 0)

    k_pair = k_ref[0]
    k_ms = _dot_rhs01(k_pair * k_pair, _pair_block_ones()) * (1.0 / HD)
    kn = k_pair * lax.rsqrt(k_ms + RMS_EPS) * kg_ref[...]
    kmean[pl.ds(own, 1), :] = jnp.mean(kn, axis=0, keepdims=True)

    q_t = q_ref[0].T
    v_t = v_ref[0].T
    row_t = lax.broadcasted_iota(jnp.int32, (LANES, BS), 0)
    blk = lax.broadcasted_iota(jnp.int32, (n_blocks, BS), 0)
    blk_f = blk.astype(F32)
    km = kmean[...]
    lane_km = lax.broadcasted_iota(jnp.int32, (n_blocks, LANES), 1)

    q_aug_t, k_blk, v_blk_t = [], [], []
    for hh in range(2):
        head = 2 * pair + hh
        slope = jnp.exp2(-jnp.full((1, 1), 1.0, F32) * (head + 1).astype(F32))
        base = HD if hh == 0 else 0
        hs = slice(hh * HD, (hh + 1) * HD)
        q_h = q_t[hs, :]
        q_ms = jnp.mean(q_h * q_h, axis=0, keepdims=True)
        qn_h = q_h * lax.rsqrt(q_ms + RMS_EPS) * qg_ref[hs, :] * (HD ** -0.5)

        km_h = jnp.where((lane_km >= hh * HD) & (lane_km < (hh + 1) * HD), km, 0.0)
        zeros_h = jnp.zeros((HD, BS), F32)
        qn_full = jnp.concatenate([qn_h, zeros_h] if hh == 0 else [zeros_h, qn_h], axis=0)
        gate = _dot(km_h, qn_full, precision=HIGHEST)
        g = jnp.where(blk < own, gate, -jnp.inf)
        sel = blk == own
        for _ in range(MOBA_TOPK):
            mx = jnp.max(g, axis=0, keepdims=True)
            first = jnp.min(jnp.where(g == mx, blk_f, 1e9), axis=0, keepdims=True)
            pick = (blk_f == first) & (mx > -jnp.inf)
            sel = sel | pick
            g = jnp.where(pick, -jnp.inf, g)
        bias = jnp.where(sel, 0.0, NEG)
        r8 = lax.broadcasted_iota(jnp.int32, (AUX_GATE, BS), 0)
        ones2 = jnp.where(r8 < 2, 1.0, 0.0)
        aux = jnp.concatenate([ones2, bias, jnp.zeros((HD - AUX_GATE - n_blocks, BS), F32)], axis=0)
        parts = [qn_h, aux] if hh == 0 else [aux, qn_h]
        q_aug_t.append(jnp.concatenate(parts, axis=0).astype(BF16))

        in_head = (lane >= hh * HD) & (lane < (hh + 1) * HD)
        k_aux = jnp.where(lane == base, slope * rowi.astype(F32),
                          jnp.where(lane == base + 1, slope * (own * BS).astype(F32),
                                    jnp.where(lane - (base + AUX_GATE) == own, 1.0, 0.0)))
        k_own = jnp.where(in_head, kn, k_aux).astype(BF16)
        in_head_t = (row_t >= hh * HD) & (row_t < (hh + 1) * HD)
        v_own_t = jnp.where(in_head_t, v_t, jnp.where(row_t == base, 1.0, 0.0)).astype(BF16)
        kaug[hh, own] = k_own
        vaug_t[hh, own] = v_own_t
        k_blk.append(k_own)
        v_blk_t.append(v_own_t)

    r2 = lax.broadcasted_iota(jnp.int32, (BS, BS), 0)
    c2 = lax.broadcasted_iota(jnp.int32, (BS, BS), 1)
    causal_t = r2 <= c2
    carry = []
    for hh in range(2):
        s = jnp.where(causal_t, _dot(k_blk[hh], q_aug_t[hh]), NEG)
        m = jnp.max(s, axis=0, keepdims=True)
        p = jnp.exp(s - m)
        carry += [m, _dot(v_blk_t[hh], p.astype(BF16))]

    def scores(hh, j):
        return _dot(kaug[hh, j], q_aug_t[hh])

    def update(hh, j, s, m, acc):
        m_new = jnp.maximum(m, jnp.max(s, axis=0, keepdims=True))
        p = jnp.exp(s - m_new)
        return m_new, jnp.exp(m - m_new) * acc + _dot(vaug_t[hh, j], p.astype(BF16))

    def score_pair(buf, t):
        for u in range(2):
            for hh in range(2):
                buf[u, hh] = scores(hh, 2 * t + u)

    def update_pair(buf, t, carry):
        carry = list(carry)
        for u in range(2):
            for hh in range(2):
                carry[2 * hh], carry[2 * hh + 1] = update(
                    hh, 2 * t + u, buf[u, hh], carry[2 * hh], carry[2 * hh + 1])
        return tuple(carry)

    half = own // 2
    quarter = half // 2

    @pl.when(half > 0)
    def _():
        score_pair(s_buf_a, 0)

    def body4(t, carry):
        score_pair(s_buf_b, 2 * t + 1)
        carry = update_pair(s_buf_a, 2 * t, carry)
        score_pair(s_buf_a, jnp.minimum(2 * t + 2, half - 1))
        return update_pair(s_buf_b, 2 * t + 1, carry)

    def body2(t, carry):
        return update_pair(s_buf_a, t, carry)

    def body1(j, carry):
        carry = list(carry)
        for hh in range(2):
            carry[2 * hh], carry[2 * hh + 1] = update(hh, j, scores(hh, j), carry[2 * hh], carry[2 * hh + 1])
        return tuple(carry)

    carry = lax.fori_loop(0, quarter, body4, tuple(carry))
    carry = lax.fori_loop(2 * quarter, half, body2, carry)
    carry = lax.fori_loop(2 * half, own, body1, carry)

    outs = []
    for hh in range(2):
        base = HD if hh == 0 else 0
        acc = carry[2 * hh + 1]
        outs.append(acc / acc[base:base + 1, :])
    out_t = jnp.where(row_t < HD, outs[0], outs[1])
    out_ref[0] = out_t.T.astype(out_ref.dtype)


def _moba(u3, q_g_t, k_g):
    B, S, _ = u3.shape
    BS = MOBA_BLOCK
    nb = S // BS
    assert AUX_GATE + nb <= HEAD_DIM and nb % 8 == 0
    return pl.pallas_call(
        functools.partial(_moba_kernel, n_blocks=nb),
        grid=(B, N_PAIRS, nb),
        in_specs=[
            pl.BlockSpec((1, BS, LANES), lambda b, p, i: (b, i, COL_MQ // LANES + p)),
            pl.BlockSpec((1, BS, LANES), lambda b, p, i: (b, i, COL_MK // LANES + p)),
            pl.BlockSpec((1, BS, LANES), lambda b, p, i: (b, i, COL_MV // LANES + p)),
            pl.BlockSpec((LANES, BS), lambda b, p, i: (0, 0)),
            pl.BlockSpec((1, LANES), lambda b, p, i: (0, 0)),
        ],
        out_specs=pl.BlockSpec((1, BS, LANES), lambda b, p, i: (b, i, p)),
        out_shape=jax.ShapeDtypeStruct((B, S, GROUP_WIDTH), BF16),
        scratch_shapes=[
            pltpu.VMEM((2, nb, BS, LANES), BF16),
            pltpu.VMEM((2, nb, LANES, BS), BF16),
            pltpu.VMEM((nb, LANES), F32),
            pltpu.VMEM((2, 2, BS, BS), F32),
            pltpu.VMEM((2, 2, BS, BS), F32),
        ],
        compiler_params=_cparams(("parallel", "parallel", "arbitrary")),
        name="moba",
    )(u3, u3, u3, q_g_t, k_g)


def _mixing(x, mix_norm_g, w_in, conv_w, conv_b, igate_b, fgate_b, mlstm_head_g, q_norm_g, k_norm_g):
    B, S, D = x.shape
    W = GROUP_WIDTH
    n_gate = 2 * N_HEADS
    w_main = jnp.concatenate([w_in[:, :4 * W], w_in[:, 4 * W + n_gate:]], axis=1).astype(BF16)
    w_gates = jnp.pad(w_in[:, 4 * W:4 * W + n_gate], ((0, 0), (0, LANES - n_gate))).astype(BF16)
    u, gates = _in_proj(x.reshape(B * S, D), mix_norm_g.reshape(1, D), w_main, w_gates)
    u3 = u.reshape(B, S, MAIN_COLS)
    gates3 = gates.reshape(B, S, LANES)
    gate_b = jnp.pad(jnp.concatenate([igate_b, fgate_b]), (0, LANES - n_gate)).reshape(1, LANES)
    h_m = _mlstm(u3, gates3, conv_w, conv_b.reshape(1, 2 * W), gate_b, mlstm_head_g.reshape(1, W))
    qg_t = jnp.broadcast_to(jnp.concatenate([q_norm_g, q_norm_g]).reshape(LANES, 1), (LANES, MOBA_BLOCK))
    kg2 = jnp.concatenate([k_norm_g, k_norm_g]).reshape(1, LANES)
    h_b = _moba(u3, qg_t, kg2)
    return h_m, h_b


TOK_TILE = 256
EXPERT_TILE = 256
OUT_SUBTILES = 1


def _out_proj_kernel(x_ref, hm_ref, hb_ref, wo_ref, g_ref, wr_ref, br_ref,
                     x1_ref, hn_ref, route_ref, wts_ref, counts_ref, carry):
    tm = TOK_TILE
    W = GROUP_WIDTH
    step = pl.program_id(0)

    @pl.when(step == 0)
    def _():
        carry[...] = jnp.zeros_like(carry)

    eidx = lax.broadcasted_iota(jnp.int32, (N_EXPERTS, tm), 0).astype(F32)
    r2 = lax.broadcasted_iota(jnp.int32, (tm, tm), 0)
    c2 = lax.broadcasted_iota(jnp.int32, (tm, tm), 1)
    incl = (r2 <= c2).astype(BF16)
    row8 = lax.broadcasted_iota(jnp.int32, (8, tm), 0)
    rowl = lax.broadcasted_iota(jnp.int32, (LANES, tm), 0)
    seen = carry[:, 0:1]

    for sub in range(OUT_SUBTILES):
        rows = slice(sub * tm, (sub + 1) * tm)
        h_mix = jnp.concatenate([hm_ref[rows, :], hb_ref[rows, :]], axis=1)
        x1 = x_ref[rows, :] + _dot(h_mix, wo_ref[...])
        x1_ref[rows, :] = x1
        hn = x1 * lax.rsqrt(jnp.mean(x1 * x1, axis=-1, keepdims=True) + RMS_EPS) * g_ref[...]
        _store_token_tiles(hn_ref, hn, row0=sub * tm)

        logits = _dot_nt(wr_ref[...], hn, precision=HIGHEST) + br_ref[:, 0:1]
        vals, onehots, idxs = [], [], []
        l = logits
        for _ in range(TOP_K):
            mx = jnp.max(l, axis=0, keepdims=True)
            first = jnp.min(jnp.where(l == mx, eidx, 1e9), axis=0, keepdims=True)
            pick = eidx == first
            vals.append(mx)
            idxs.append(first)
            onehots.append(pick.astype(F32))
            l = jnp.where(pick, -jnp.inf, l)
        exps = [jnp.exp(v - vals[0]) for v in vals]
        inv = 1.0 / (exps[0] + exps[1] + exps[2] + exps[3])
        total = onehots[0] + onehots[1] + onehots[2] + onehots[3]
        cum = _dot(total.astype(BF16), incl)
        excl = cum - total + seen
        ranks = [jnp.sum(oh * excl, axis=0, keepdims=True) for oh in onehots]
        seen = seen + jnp.sum(total, axis=-1, keepdims=True)

        route = jnp.zeros((8, tm), F32)
        for k in range(TOP_K):
            route = jnp.where(row8 == k, idxs[k], route)
            route = jnp.where(row8 == TOP_K + k, ranks[k], route)
        route_ref[:, rows] = route.astype(jnp.int32)
        wt = jnp.zeros((LANES, tm), F32)
        for k in range(TOP_K):
            wt = jnp.where(rowl == k, exps[k] * inv, wt)
        wts_ref[rows, :] = wt.T

    carry[...] = jnp.broadcast_to(seen, carry.shape)
    counts_ref[...] = carry[...].astype(jnp.int32)


def _out_proj(x2, h_m, h_b, w_out, g, w_router_t, b_router):
    T, D = x2.shape
    tm = TOK_TILE * OUT_SUBTILES
    W = GROUP_WIDTH
    return pl.pallas_call(
        _out_proj_kernel,
        grid=(T // tm,),
        in_specs=[
            pl.BlockSpec((tm, D), lambda i: (i, 0)),
            pl.BlockSpec((tm, W), lambda i: (i, 0)),
            pl.BlockSpec((tm, W), lambda i: (i, 0)),
            pl.BlockSpec((2 * W, D), lambda i: (0, 0)),
            pl.BlockSpec((1, D), lambda i: (0, 0)),
            pl.BlockSpec((N_EXPERTS, D), lambda i: (0, 0)),
            pl.BlockSpec((N_EXPERTS, LANES), lambda i: (0, 0)),
        ],
        out_specs=[
            pl.BlockSpec((tm, D), lambda i: (i, 0)),
            pl.BlockSpec((tm * TOKEN_SUBLANES, LANES), lambda i: (i, 0)),
            pl.BlockSpec((8, tm), lambda i: (0, i)),
            pl.BlockSpec((tm, LANES), lambda i: (i, 0)),
            pl.BlockSpec((N_EXPERTS, LANES), lambda i: (0, 0)),
        ],
        out_shape=[
            jax.ShapeDtypeStruct((T, D), F32),
            jax.ShapeDtypeStruct((T * TOKEN_SUBLANES, LANES), F32),
            jax.ShapeDtypeStruct((8, T), jnp.int32),
            jax.ShapeDtypeStruct((T, LANES), F32),
            jax.ShapeDtypeStruct((N_EXPERTS, LANES), jnp.int32),
        ],
        scratch_shapes=[pltpu.VMEM((N_EXPERTS, LANES), F32)],
        compiler_params=_cparams(("arbitrary",)),
        name="out_proj",
    )(x2, h_m, h_b, w_out, g, w_router_t, b_router)


TOKEN_SUBLANES = 8


def _store_token_tiles(ref, val, row0=0):
    n, d = val.shape
    assert d == TOKEN_SUBLANES * LANES
    for s in range(TOKEN_SUBLANES):
        ref[pl.ds(row0 * TOKEN_SUBLANES + s, n, stride=TOKEN_SUBLANES), :] = val[:, s * LANES:(s + 1) * LANES]


def _load_token_tiles(ref):
    n = ref.shape[0] // TOKEN_SUBLANES
    return jnp.concatenate([ref[pl.ds(s, n, stride=TOKEN_SUBLANES), :] for s in range(TOKEN_SUBLANES)], axis=1)


def _token_tile(ref, row):
    return ref.at[pl.ds(pl.multiple_of(row * TOKEN_SUBLANES, TOKEN_SUBLANES), TOKEN_SUBLANES), :]


def _row_copy(src, s_row, dst, d_row, sem):
    return pltpu.make_async_copy(_token_tile(src, s_row), _token_tile(dst, d_row), sem)


def _dispatch_kernel(ends_ref, padded_ref, hn_ref, pos_hbm, xs_hbm, zbuf, pos_smem, pos_sem, row_sem, z_sem):
    tm = TOK_TILE
    te = EXPERT_TILE
    step = pl.program_id(0)

    @pl.when(step == 0)
    def _():
        zbuf[...] = jnp.zeros_like(zbuf)
        total = ends_ref[N_EXPERTS - 1]
        n_rows = xs_hbm.shape[0] // TOKEN_SUBLANES

        def zero_tile(start):
            rows = te * TOKEN_SUBLANES
            return pltpu.make_async_copy(
                zbuf, xs_hbm.at[pl.ds(pl.multiple_of(start * TOKEN_SUBLANES, rows), rows), :], z_sem)

        for e in range(N_EXPERTS):
            @pl.when(padded_ref[e] > 0)
            def _():
                zero_tile(ends_ref[e] - te).start()

            @pl.when(total + e * te < n_rows)
            def _():
                zero_tile(total + e * te).start()
        for e in range(N_EXPERTS):
            @pl.when(padded_ref[e] > 0)
            def _():
                zero_tile(0).wait()

            @pl.when(total + e * te < n_rows)
            def _():
                zero_tile(0).wait()

    cp = pltpu.make_async_copy(pos_hbm.at[step], pos_smem, pos_sem)
    cp.start()
    cp.wait()

    def issue(t, _):
        for k in range(TOP_K):
            _row_copy(hn_ref, t, xs_hbm, pos_smem[k * tm + t], row_sem).start(priority=k % 2)
        return 0

    lax.fori_loop(0, tm, issue, 0)

    def drain(t, _):
        for k in range(TOP_K):
            _row_copy(hn_ref, 0, xs_hbm, 0, row_sem).wait()
        return 0

    lax.fori_loop(0, tm, drain, 0)


def _dispatch(hn, pos_tiles, ends, padded, n_rows):
    tm = TOK_TILE
    T = hn.shape[0] // TOKEN_SUBLANES
    grid_spec = pltpu.PrefetchScalarGridSpec(
        num_scalar_prefetch=2,
        grid=(T // tm,),
        in_specs=[
            pl.BlockSpec((tm * TOKEN_SUBLANES, LANES), lambda i, e, p: (i, 0)),
            pl.BlockSpec(memory_space=pl.ANY),
        ],
        out_specs=pl.BlockSpec(memory_space=pl.ANY),
        scratch_shapes=[
            pltpu.VMEM((EXPERT_TILE * TOKEN_SUBLANES, LANES), F32),
            pltpu.SMEM((TOP_K * tm,), jnp.int32),
            pltpu.SemaphoreType.DMA,
            pltpu.SemaphoreType.DMA,
            pltpu.SemaphoreType.DMA,
        ],
    )
    return pl.pallas_call(
        _dispatch_kernel,
        grid_spec=grid_spec,
        out_shape=jax.ShapeDtypeStruct((n_rows * TOKEN_SUBLANES, LANES), F32),
        compiler_params=_cparams(("arbitrary",)),
        name="dispatch",
    )(ends, padded, hn, pos_tiles)


def _experts_kernel(te_ref, nu_ref, x_ref, wgu_ref, bgu_ref, wdn_ref, bdn_ref, y_ref, wgu_b, wdn_b):
    i = pl.program_id(0)
    F = wdn_ref.shape[1]
    prev = te_ref[jnp.maximum(i - 1, 0)]
    fresh = (i == 0) | (te_ref[i] != prev)

    @pl.when(fresh & (i < nu_ref[0]))
    def _():
        wgu_b[...] = wgu_ref[0].astype(BF16)
        wdn_b[...] = wdn_ref[0].astype(BF16)

    @pl.when(i < nu_ref[0])
    def _():
        xb = _load_token_tiles(x_ref).astype(BF16)
        hu = _dot(xb, wgu_b[...]) + bgu_ref[0]
        glu = jnp.minimum(hu[:, :F], SWIGLU_LIMIT)
        lin = jnp.clip(hu[:, F:], -SWIGLU_LIMIT, SWIGLU_LIMIT)
        act = glu * _sigmoid(SWIGLU_ALPHA * glu) * (lin + 1.0)
        _store_token_tiles(y_ref, _dot(act.astype(BF16), wdn_b[...]) + bdn_ref[0])

    @pl.when(i >= nu_ref[0])
    def _():
        y_ref[...] = jnp.zeros_like(y_ref)


def _experts(tile_expert, n_used, x_sorted, w_gu, b_gu, w_dn, b_dn):
    P = x_sorted.shape[0] // TOKEN_SUBLANES
    E, D, F2 = w_gu.shape
    F = F2 // 2
    tm = EXPERT_TILE
    n_tiles = P // tm

    def row_map(i, te, nu):
        return (jnp.maximum(jnp.minimum(i, nu[0] - 1), 0), 0)

    def exp_map(i, te, nu):
        return (te[i], 0, 0)

    grid_spec = pltpu.PrefetchScalarGridSpec(
        num_scalar_prefetch=2,
        grid=(n_tiles,),
        in_specs=[
            pl.BlockSpec((tm * TOKEN_SUBLANES, LANES), row_map),
            pl.BlockSpec((1, D, F2), exp_map),
            pl.BlockSpec((1, 1, F2), exp_map),
            pl.BlockSpec((1, F, D), exp_map),
            pl.BlockSpec((1, 1, D), exp_map),
        ],
        out_specs=pl.BlockSpec((tm * TOKEN_SUBLANES, LANES), lambda i, te, nu: (i, 0)),
        scratch_shapes=[pltpu.VMEM((D, F2), BF16), pltpu.VMEM((F, D), BF16)],
    )
    return pl.pallas_call(
        _experts_kernel,
        grid_spec=grid_spec,
        out_shape=jax.ShapeDtypeStruct((P * TOKEN_SUBLANES, LANES), F32),
        compiler_params=_cparams(("arbitrary",)),
        name="experts",
    )(tile_expert, n_used, x_sorted, w_gu, b_gu.reshape(E, 1, F2), w_dn, b_dn.reshape(E, 1, D))


def _combine_kernel(x1_ref, wts_ref, p_ref, g_ref, wg_ref, wp_ref, pos_hbm, ys_hbm,
                    out_ref, ybuf, pos_smem, pos_sem, row_sem):
    tm = TOK_TILE
    step = pl.program_id(0)
    cp = pltpu.make_async_copy(pos_hbm.at[step], pos_smem, pos_sem)
    cp.start()
    cp.wait()

    def issue(t, _):
        for k in range(TOP_K):
            _row_copy(ys_hbm, pos_smem[k * tm + t], ybuf.at[k], t, row_sem).start(priority=k % 2)
        return 0

    lax.fori_loop(0, tm, issue, 0)

    def drain(t, _):
        for k in range(TOP_K):
            _row_copy(ys_hbm, 0, ybuf.at[k], 0, row_sem).wait()
        return 0

    lax.fori_loop(0, tm, drain, 0)

    w = wts_ref[...]
    x2 = x1_ref[...]
    for k in range(TOP_K):
        x2 = x2 + w[:, k:k + 1] * _load_token_tiles(ybuf.at[k])
    hn = x2 * lax.rsqrt(jnp.mean(x2 * x2, axis=-1, keepdims=True) + RMS_EPS) * g_ref[...]
    gate = _sigmoid(_dot(hn.astype(BF16), wg_ref[...]))
    out_ref[...] = x2 + gate * _dot(p_ref[...].astype(BF16), wp_ref[...])


def _combine(x1, wts, p2, g, w_gate, w_proj, pos_tiles, y_sorted):
    T, D = x1.shape
    tm = TOK_TILE
    PD = p2.shape[1]
    return pl.pallas_call(
        _combine_kernel,
        grid=(T // tm,),
        in_specs=[
            pl.BlockSpec((tm, D), lambda i: (i, 0)),
            pl.BlockSpec((tm, LANES), lambda i: (i, 0)),
            pl.BlockSpec((tm, PD), lambda i: (i, 0)),
            pl.BlockSpec((1, D), lambda i: (0, 0)),
            pl.BlockSpec((D, D), lambda i: (0, 0)),
            pl.BlockSpec((PD, D), lambda i: (0, 0)),
            pl.BlockSpec(memory_space=pl.ANY),
            pl.BlockSpec(memory_space=pl.ANY),
        ],
        out_specs=pl.BlockSpec((tm, D), lambda i: (i, 0)),
        out_shape=jax.ShapeDtypeStruct((T, D), F32),
        scratch_shapes=[
            pltpu.VMEM((TOP_K, tm * TOKEN_SUBLANES, LANES), F32),
            pltpu.SMEM((TOP_K * tm,), jnp.int32),
            pltpu.SemaphoreType.DMA,
            pltpu.SemaphoreType.DMA,
        ],
        compiler_params=_cparams(("arbitrary",)),
        name="combine",
    )(x1, wts, p2, g, w_gate, w_proj, pos_tiles, y_sorted)


def _routing_tables(route, counts, n_tokens):
    te = EXPERT_TILE
    tm = TOK_TILE
    idx = route[:TOP_K]
    rank = route[TOP_K:]
    cnt = counts[:, 0]
    padded = ((cnt + te - 1) // te) * te
    ends = jnp.cumsum(padded)
    offs = ends - padded
    e_ids = jnp.arange(N_EXPERTS, dtype=jnp.int32)[:, None, None]
    pos = jnp.sum(jnp.where(idx[None] == e_ids, offs[:, None, None], 0), axis=0) + rank
    pos_tiles = pos.reshape(TOP_K, n_tokens // tm, tm).transpose(1, 0, 2).reshape(n_tokens // tm, TOP_K * tm)
    n_rows = n_tokens * TOP_K + N_EXPERTS * te
    starts = jnp.arange(n_rows // te, dtype=jnp.int32) * te
    tile_expert = jnp.minimum(jnp.sum((starts[:, None] >= ends[None, :]).astype(jnp.int32), axis=1), N_EXPERTS - 1)
    n_used = (ends[-1] // te).astype(jnp.int32).reshape(1)
    return (pos_tiles.astype(jnp.int32), tile_expert.astype(jnp.int32), n_used,
            ends.astype(jnp.int32), padded.astype(jnp.int32), n_rows)


def kernel(x, p, mix_norm_g, w_in, conv_w, conv_b, igate_b, fgate_b, mlstm_head_g, q_norm_g, k_norm_g, w_out, ffn_norm_g, w_router, b_router, w_gate_up, b_gate_up, w_down, b_down, ple_norm_g, w_ple_gate, w_ple_proj):
    B, S, D = x.shape
    T = B * S
    depth = mix_norm_g.shape[0]
    for l in range(depth):
        h_m, h_b = _mixing(x, mix_norm_g[l], w_in[l], conv_w[l], conv_b[l], igate_b[l], fgate_b[l],
                           mlstm_head_g[l], q_norm_g[l], k_norm_g[l])
        b_r = jnp.broadcast_to(b_router[l].reshape(N_EXPERTS, 1), (N_EXPERTS, LANES))
        x1, hn, route, wts, counts = _out_proj(
            x.reshape(T, D), h_m.reshape(T, GROUP_WIDTH), h_b.reshape(T, GROUP_WIDTH),
            w_out[l].astype(BF16), ffn_norm_g[l].reshape(1, D), w_router[l].T, b_r)
        pos_tiles, tile_expert, n_used, ends, padded, n_rows = _routing_tables(route, counts, T)
        x_sorted = _dispatch(hn, pos_tiles, ends, padded, n_rows)
        y_sorted = _experts(tile_expert, n_used, x_sorted, w_gate_up[l], b_gate_up[l], w_down[l], b_down[l])
        out = _combine(x1, wts, p[l].reshape(T, -1), ple_norm_g[l].reshape(1, D),
                       w_ple_gate[l].astype(BF16), w_ple_proj[l].astype(BF16), pos_tiles, y_sorted)
        x = out.reshape(B, S, D)
    return x
```

```python
import functools

import jax
import jax.numpy as jnp
from jax import lax
from jax.experimental import pallas as pl
from jax.experimental.pallas import tpu as pltpu

F32 = jnp.float32
BF16 = jnp.bfloat16
HIGHEST = lax.Precision.HIGHEST

HEAD_DIM = 64
LANES = 128
N_HEADS = 8
GROUP_WIDTH = N_HEADS * HEAD_DIM
N_PAIRS = N_HEADS // 2
MLSTM_CHUNK = 128
CONV_WIDTH = 4
MOBA_BLOCK = 256
MOBA_TOPK = 3
N_EXPERTS = 32
TOP_K = 4
SWIGLU_LIMIT = 7.0
SWIGLU_ALPHA = 1.702
RMS_EPS = 1e-6
NEG = -1e30
VMEM_LIMIT = 56 * 1024 * 1024

COL_QK = 0
COL_V = 2 * GROUP_WIDTH
COL_O = 3 * GROUP_WIDTH
COL_MQ = 4 * GROUP_WIDTH
COL_MK = 5 * GROUP_WIDTH
COL_MV = 6 * GROUP_WIDTH
MAIN_COLS = 7 * GROUP_WIDTH


def _cparams(sem):
    return pltpu.CompilerParams(dimension_semantics=sem, vmem_limit_bytes=VMEM_LIMIT)


def _dot(a, b, **kw):
    return jnp.dot(a, b, preferred_element_type=F32, **kw)


def _dot_nt(a, b, **kw):
    return lax.dot_general(a, b, (((1,), (1,)), ((), ())), preferred_element_type=F32, **kw)


def _split3(x):
    hi = x.astype(BF16)
    r = x - hi.astype(F32)
    mid = r.astype(BF16)
    lo = (r - mid.astype(F32)).astype(BF16)
    return hi, mid, lo


def _dot_rhs01(a, b01):
    b = b01.astype(BF16)
    hi, mid, lo = _split3(a)
    return _dot(hi, b) + _dot(mid, b) + _dot(lo, b)


def _dot_lhs01(a01, b):
    a = a01.astype(BF16)
    hi, mid, lo = _split3(b)
    return _dot(a, hi) + _dot(a, mid) + _dot(a, lo)


def _sigmoid(x):
    return 1.0 / (1.0 + jnp.exp(-x))


def _lane_half_mask(shape):
    return lax.broadcasted_iota(jnp.int32, shape, len(shape) - 1) < HEAD_DIM


def _pair_block_ones():
    r = lax.broadcasted_iota(jnp.int32, (LANES, LANES), 0) // HEAD_DIM
    c = lax.broadcasted_iota(jnp.int32, (LANES, LANES), 1) // HEAD_DIM
    return r == c


def _in_proj_kernel(x_ref, g_ref, w_ref, wg_ref, u_ref, gates_ref, *, n_chunk):
    x = x_ref[...]
    h = x * lax.rsqrt(jnp.mean(x * x, axis=-1, keepdims=True) + RMS_EPS) * g_ref[...]
    hb = h.astype(BF16)
    for c in range(MAIN_COLS // n_chunk):
        u_ref[:, c * n_chunk:(c + 1) * n_chunk] = _dot(hb, w_ref[:, c * n_chunk:(c + 1) * n_chunk])
    gates_ref[...] = _dot(hb, wg_ref[...])


def _in_proj(x2, g, w_main, w_gates, tm=512):
    T, D = x2.shape
    return pl.pallas_call(
        functools.partial(_in_proj_kernel, n_chunk=GROUP_WIDTH),
        grid=(T // tm,),
        in_specs=[
            pl.BlockSpec((tm, D), lambda i: (i, 0)),
            pl.BlockSpec((1, D), lambda i: (0, 0)),
            pl.BlockSpec((D, MAIN_COLS), lambda i: (0, 0)),
            pl.BlockSpec((D, LANES), lambda i: (0, 0)),
        ],
        out_specs=[
            pl.BlockSpec((tm, MAIN_COLS), lambda i: (i, 0)),
            pl.BlockSpec((tm, LANES), lambda i: (i, 0)),
        ],
        out_shape=[
            jax.ShapeDtypeStruct((T, MAIN_COLS), F32),
            jax.ShapeDtypeStruct((T, LANES), F32),
        ],
        compiler_params=_cparams(("parallel",)),
        name="in_proj",
    )(x2, g, w_main, w_gates)


def _mlstm_kernel(qk_ref, v_ref, o_ref, gates_ref, cw_ref, cb_ref, gb_ref, hg_ref,
                  out_ref, ext_ref, c_state, n_state, m_state):
    L = MLSTM_CHUNK
    W = GROUP_WIDTH
    chunk = pl.program_id(1)

    @pl.when(chunk == 0)
    def _():
        ext_ref[0:8, :] = jnp.zeros((8, 2 * W), F32)
        c_state[...] = jnp.zeros_like(c_state)
        n_state[...] = jnp.zeros_like(n_state)
        m_state[...] = jnp.zeros_like(m_state)

    ext_ref[8:8 + L, :] = qk_ref[0]
    y = cb_ref[...] + ext_ref[8:8 + L, :] * cw_ref[3:4, :]
    for j in range(1, CONV_WIDTH):
        y = y + ext_ref[8 - j:8 - j + L, :] * cw_ref[3 - j:4 - j, :]
    ext_ref[0:8, :] = ext_ref[L:L + 8, :]
    a = y * _sigmoid(y)
    q_all = a[:, :W] * (HEAD_DIM ** -0.5)
    k_all = a[:, W:]

    pre = gates_ref[0] + gb_ref[...]
    lane = lax.broadcasted_iota(jnp.int32, (L, LANES), 1)
    logsig = jnp.minimum(pre, 0.0) - jnp.log(1.0 + jnp.exp(-jnp.abs(pre)))
    G = jnp.where(lane < N_HEADS, pre, logsig)
    row = lax.broadcasted_iota(jnp.int32, (L, L), 0)
    col = lax.broadcasted_iota(jnp.int32, (L, L), 1)
    causal = col <= row
    Bc = _dot_lhs01(causal, G)
    GT = G.T
    BT = Bc.T

    first_half = _lane_half_mask((L, LANES))
    first_half_row = _lane_half_mask((1, LANES))
    blockdiag = _pair_block_ones()

    for p in range(N_PAIRS):
        sl = slice(p * LANES, (p + 1) * LANES)
        q_pair = q_all[:, sl]
        k_pair = k_all[:, sl]
        v_pair = v_ref[0][:, sl]
        qb = q_pair.astype(BF16)
        kb = k_pair.astype(BF16)
        vb = v_pair.astype(BF16)
        Cp = c_state[p]
        Np = n_state[p]
        q_c = _dot(qb, Cp.astype(BF16))
        q_n = _dot(qb, Np.astype(BF16))
        h_halves = []
        w_cols = []
        decays = []
        for hh in range(2):
            h = 2 * p + hh
            bcol = Bc[:, N_HEADS + h:N_HEADS + h + 1]
            brow = BT[N_HEADS + h:N_HEADS + h + 1, :]
            irow = GT[h:h + 1, :]
            icol = G[:, h:h + 1]
            m_prev = m_state[h:h + 1, 0:1]
            logD = jnp.where(causal, bcol - brow + irow, -jnp.inf)
            inter = bcol + m_prev
            m_t = jnp.maximum(inter, jnp.max(logD, axis=-1, keepdims=True))
            Dm = jnp.exp(logD - m_t)
            sc = jnp.exp(inter - m_t)
            in_head = first_half if hh == 0 else jnp.logical_not(first_half)
            q_h = jnp.where(in_head, q_pair, 0.0).astype(BF16)
            s = _dot_nt(q_h, kb) * Dm
            num = _dot(s.astype(BF16), vb) + sc * q_c
            den = jnp.sum(s, axis=-1, keepdims=True) + sc * q_n
            h_halves.append(num / jnp.maximum(jnp.abs(den), jnp.exp(-m_t)))
            bL = bcol[L - 1:L, :]
            logw = bL - bcol + icol
            m_new = jnp.maximum(bL + m_prev, jnp.max(logw, axis=0, keepdims=True))
            w_cols.append(jnp.exp(logw - m_new))
            decays.append(jnp.exp(bL + m_prev - m_new))
            m_state[h:h + 1, :] = jnp.broadcast_to(m_new, (1, LANES))
        h_pair = jnp.where(first_half, h_halves[0], h_halves[1])
        wk = jnp.where(first_half, w_cols[0], w_cols[1]) * k_pair
        wkT = wk.T
        dec = jnp.where(first_half_row, decays[0], decays[1])
        c_state[p] = dec * Cp + jnp.where(blockdiag, _dot(wkT.astype(BF16), vb), 0.0)
        n_state[p] = dec * Np + jnp.where(blockdiag, jnp.sum(wkT, axis=-1, keepdims=True), 0.0)
        ms = _dot_rhs01(h_pair * h_pair, blockdiag) * (1.0 / HEAD_DIM)
        hn = h_pair * lax.rsqrt(ms + RMS_EPS) * hg_ref[:, sl]
        out_ref[0, :, sl] = (hn * _sigmoid(o_ref[0][:, sl])).astype(out_ref.dtype)


def _mlstm(u3, gates3, conv_w, conv_b, gate_b, head_g):
    B, S, _ = u3.shape
    L = MLSTM_CHUNK
    W = GROUP_WIDTH
    return pl.pallas_call(
        _mlstm_kernel,
        grid=(B, S // L),
        in_specs=[
            pl.BlockSpec((1, L, 2 * W), lambda b, c: (b, c, COL_QK // (2 * W))),
            pl.BlockSpec((1, L, W), lambda b, c: (b, c, COL_V // W)),
            pl.BlockSpec((1, L, W), lambda b, c: (b, c, COL_O // W)),
            pl.BlockSpec((1, L, LANES), lambda b, c: (b, c, 0)),
            pl.BlockSpec((CONV_WIDTH, 2 * W), lambda b, c: (0, 0)),
            pl.BlockSpec((1, 2 * W), lambda b, c: (0, 0)),
            pl.BlockSpec((1, LANES), lambda b, c: (0, 0)),
            pl.BlockSpec((1, W), lambda b, c: (0, 0)),
        ],
        out_specs=pl.BlockSpec((1, L, W), lambda b, c: (b, c, 0)),
        out_shape=jax.ShapeDtypeStruct((B, S, W), BF16),
        scratch_shapes=[
            pltpu.VMEM((L + 8, 2 * W), F32),
            pltpu.VMEM((N_PAIRS, LANES, LANES), F32),
            pltpu.VMEM((N_PAIRS, LANES, LANES), F32),
            pltpu.VMEM((N_HEADS, LANES), F32),
        ],
        compiler_params=_cparams(("parallel", "arbitrary")),
        name="mlstm",
    )(u3, u3, u3, gates3, conv_w, conv_b, gate_b, head_g)


AUX_GATE = 8


def _moba_kernel(q_ref, k_ref, v_ref, qg_ref, kg_ref, out_ref, kaug, vaug_t, kmean, s_buf_a, s_buf_b,
                 *, n_blocks):
    BS = MOBA_BLOCK
    HD = HEAD_DIM
    pair = pl.program_id(1)
    own = pl.program_id(2)

    lane = lax.broadcasted_iota(jnp.int32, (BS, LANES), 1)
    rowi = lax.broadcasted_iota(jnp.int32, (BS, LANES), 0)

    @pl.when(own == 0)
    def _():
        kmean[...] = jnp.zeros_like(kmean)
        for hh in range(2):
            flag_lane = (HD if hh == 0 else 0) + AUX_GATE + n_blocks
            kaug[hh, n_blocks] = jnp.where(lane == flag_lane, 1.0, 0.0).astype(BF16)
            vaug_t[hh, n_blocks] = jnp.zeros((LANES, BS), BF16)

    k_pair = k_ref[0]
    k_ms = _dot_rhs01(k_pair * k_pair, _pair_block_ones()) * (1.0 / HD)
    kn = k_pair * lax.rsqrt(k_ms + RMS_EPS) * kg_ref[...]
    kmean[pl.ds(own, 1), :] = jnp.mean(kn, axis=0, keepdims=True)

    q_t = q_ref[0].T
    v_t = v_ref[0].T
    row_t = lax.broadcasted_iota(jnp.int32, (LANES, BS), 0)
    blk = lax.broadcasted_iota(jnp.int32, (n_blocks, BS), 0)
    blk_f = blk.astype(F32)
    km = kmean[...]
    lane_km = lax.broadcasted_iota(jnp.int32, (n_blocks, LANES), 1)

    q_aug_t, k_blk, v_blk_t = [], [], []
    for hh in range(2):
        head = 2 * pair + hh
        slope = jnp.exp2(-jnp.full((1, 1), 1.0, F32) * (head + 1).astype(F32))
        base = HD if hh == 0 else 0
        hs = slice(hh * HD, (hh + 1) * HD)
        q_h = q_t[hs, :]
        q_ms = jnp.mean(q_h * q_h, axis=0, keepdims=True)
        qn_h = q_h * lax.rsqrt(q_ms + RMS_EPS) * qg_ref[hs, :] * (HD ** -0.5)

        km_h = jnp.where((lane_km >= hh * HD) & (lane_km < (hh + 1) * HD), km, 0.0)
        zeros_h = jnp.zeros((HD, BS), F32)
        qn_full = jnp.concatenate([qn_h, zeros_h] if hh == 0 else [zeros_h, qn_h], axis=0)
        gate = _dot(km_h, qn_full, precision=HIGHEST)
        g = jnp.where(blk < own, gate, -jnp.inf)
        sel = blk == own
        for _ in range(MOBA_TOPK):
            mx = jnp.max(g, axis=0, keepdims=True)
            first = jnp.min(jnp.where(g == mx, blk_f, 1e9), axis=0, keepdims=True)
            pick = (blk_f == first) & (mx > -jnp.inf)
            sel = sel | pick
            g = jnp.where(pick, -jnp.inf, g)
        bias = jnp.where(sel, 0.0, NEG)
        r8 = lax.broadcasted_iota(jnp.int32, (AUX_GATE, BS), 0)
        ones2 = jnp.where(r8 < 2, 1.0, 0.0)
        pad_flag = jnp.where(r8 == 0, NEG, 0.0)
        aux = jnp.concatenate(
            [ones2, bias, pad_flag, jnp.zeros((HD - 2 * AUX_GATE - n_blocks, BS), F32)], axis=0)
        parts = [qn_h, aux] if hh == 0 else [aux, qn_h]
        q_aug_t.append(jnp.concatenate(parts, axis=0).astype(BF16))

        in_head = (lane >= hh * HD) & (lane < (hh + 1) * HD)
        k_aux = jnp.where(lane == base, slope * rowi.astype(F32),
                          jnp.where(lane == base + 1, slope * (own * BS).astype(F32),
                                    jnp.where(lane - (base + AUX_GATE) == own, 1.0, 0.0)))
        k_own = jnp.where(in_head, kn, k_aux).astype(BF16)
        in_head_t = (row_t >= hh * HD) & (row_t < (hh + 1) * HD)
        v_own_t = jnp.where(in_head_t, v_t, jnp.where(row_t == base, 1.0, 0.0)).astype(BF16)
        kaug[hh, own] = k_own
        vaug_t[hh, own] = v_own_t
        k_blk.append(k_own)
        v_blk_t.append(v_own_t)

    def scores(hh, j):
        return _dot(kaug[hh, j], q_aug_t[hh])

    def update(hh, j, s, m, acc):
        m_new = jnp.maximum(m, jnp.max(s, axis=0, keepdims=True))
        p = jnp.exp(s - m_new)
        return m_new, jnp.exp(m - m_new) * acc + _dot(vaug_t[hh, j], p.astype(BF16))

    def block_of(t, u):
        idx = 2 * t + u
        return jnp.where(idx == 0, own, jnp.where(idx <= own, idx - 1, n_blocks))

    def score_pair(buf, t):
        for u in range(2):
            for hh in range(2):
                buf[u, hh] = scores(hh, block_of(t, u))

    def update_pair(buf, t, carry):
        carry = list(carry)
        for u in range(2):
            for hh in range(2):
                carry[2 * hh], carry[2 * hh + 1] = update(
                    hh, block_of(t, u), buf[u, hh], carry[2 * hh], carry[2 * hh + 1])
        return tuple(carry)

    n_pairs = (own + 2) // 2
    n_quads = n_pairs // 2

    r2 = lax.broadcasted_iota(jnp.int32, (BS, BS), 0)
    c2 = lax.broadcasted_iota(jnp.int32, (BS, BS), 1)
    causal_t = r2 <= c2
    for hh in range(2):
        s_buf_a[0, hh] = jnp.where(causal_t, _dot(k_blk[hh], q_aug_t[hh]), NEG)
        s_buf_a[1, hh] = scores(hh, block_of(0, 1))
    carry = []
    for hh in range(2):
        carry += [jnp.full((1, BS), NEG, F32), jnp.zeros((LANES, BS), F32)]

    def body4(t, carry):
        score_pair(s_buf_b, 2 * t + 1)
        carry = update_pair(s_buf_a, 2 * t, carry)
        score_pair(s_buf_a, jnp.minimum(2 * t + 2, n_pairs - 1))
        return update_pair(s_buf_b, 2 * t + 1, carry)

    def body2(t, carry):
        return update_pair(s_buf_a, t, carry)

    carry = lax.fori_loop(0, n_quads, body4, tuple(carry))
    carry = lax.fori_loop(2 * n_quads, n_pairs, body2, carry)

    outs = []
    for hh in range(2):
        base = HD if hh == 0 else 0
        acc = carry[2 * hh + 1]
        outs.append(acc / acc[base:base + 1, :])
    out_t = jnp.where(row_t < HD, outs[0], outs[1])
    out_ref[0] = out_t.T.astype(out_ref.dtype)


def _moba(u3, q_g_t, k_g):
    B, S, _ = u3.shape
    BS = MOBA_BLOCK
    nb = S // BS
    assert 2 * AUX_GATE + nb <= HEAD_DIM and nb % 8 == 0
    return pl.pallas_call(
        functools.partial(_moba_kernel, n_blocks=nb),
        grid=(B, N_PAIRS, nb),
        in_specs=[
            pl.BlockSpec((1, BS, LANES), lambda b, p, i: (b, i, COL_MQ // LANES + p)),
            pl.BlockSpec((1, BS, LANES), lambda b, p, i: (b, i, COL_MK // LANES + p)),
            pl.BlockSpec((1, BS, LANES), lambda b, p, i: (b, i, COL_MV // LANES + p)),
            pl.BlockSpec((LANES, BS), lambda b, p, i: (0, 0)),
            pl.BlockSpec((1, LANES), lambda b, p, i: (0, 0)),
        ],
        out_specs=pl.BlockSpec((1, BS, LANES), lambda b, p, i: (b, i, p)),
        out_shape=jax.ShapeDtypeStruct((B, S, GROUP_WIDTH), BF16),
        scratch_shapes=[
            pltpu.VMEM((2, nb + 1, BS, LANES), BF16),
            pltpu.VMEM((2, nb + 1, LANES, BS), BF16),
            pltpu.VMEM((nb, LANES), F32),
            pltpu.VMEM((2, 2, BS, BS), F32),
            pltpu.VMEM((2, 2, BS, BS), F32),
        ],
        compiler_params=_cparams(("parallel", "parallel", "arbitrary")),
        name="moba",
    )(u3, u3, u3, q_g_t, k_g)


def _mixing(x, mix_norm_g, w_in, conv_w, conv_b, igate_b, fgate_b, mlstm_head_g, q_norm_g, k_norm_g):
    B, S, D = x.shape
    W = GROUP_WIDTH
    n_gate = 2 * N_HEADS
    w_main = jnp.concatenate([w_in[:, :4 * W], w_in[:, 4 * W + n_gate:]], axis=1).astype(BF16)
    w_gates = jnp.pad(w_in[:, 4 * W:4 * W + n_gate], ((0, 0), (0, LANES - n_gate))).astype(BF16)
    u, gates = _in_proj(x.reshape(B * S, D), mix_norm_g.reshape(1, D), w_main, w_gates)
    u3 = u.reshape(B, S, MAIN_COLS)
    gates3 = gates.reshape(B, S, LANES)
    gate_b = jnp.pad(jnp.concatenate([igate_b, fgate_b]), (0, LANES - n_gate)).reshape(1, LANES)
    h_m = _mlstm(u3, gates3, conv_w, conv_b.reshape(1, 2 * W), gate_b, mlstm_head_g.reshape(1, W))
    qg_t = jnp.broadcast_to(jnp.concatenate([q_norm_g, q_norm_g]).reshape(LANES, 1), (LANES, MOBA_BLOCK))
    kg2 = jnp.concatenate([k_norm_g, k_norm_g]).reshape(1, LANES)
    h_b = _moba(u3, qg_t, kg2)
    return h_m, h_b


TOK_TILE = 256
EXPERT_TILE = 256
OUT_SUBTILES = 1


def _out_proj_kernel(x_ref, hm_ref, hb_ref, wo_ref, g_ref, wr_ref, br_ref,
                     x1_ref, hn_ref, route_ref, wts_ref, counts_ref, carry):
    tm = TOK_TILE
    W = GROUP_WIDTH
    step = pl.program_id(0)

    @pl.when(step == 0)
    def _():
        carry[...] = jnp.zeros_like(carry)

    eidx = lax.broadcasted_iota(jnp.int32, (N_EXPERTS, tm), 0).astype(F32)
    r2 = lax.broadcasted_iota(jnp.int32, (tm, tm), 0)
    c2 = lax.broadcasted_iota(jnp.int32, (tm, tm), 1)
    incl = (r2 <= c2).astype(BF16)
    row8 = lax.broadcasted_iota(jnp.int32, (8, tm), 0)
    rowl = lax.broadcasted_iota(jnp.int32, (LANES, tm), 0)
    seen = carry[:, 0:1]

    for sub in range(OUT_SUBTILES):
        rows = slice(sub * tm, (sub + 1) * tm)
        h_mix = jnp.concatenate([hm_ref[rows, :], hb_ref[rows, :]], axis=1)
        x1 = x_ref[rows, :] + _dot(h_mix, wo_ref[...])
        x1_ref[rows, :] = x1
        hn = x1 * lax.rsqrt(jnp.mean(x1 * x1, axis=-1, keepdims=True) + RMS_EPS) * g_ref[...]
        _store_token_tiles(hn_ref, hn, row0=sub * tm)

        logits = _dot_nt(wr_ref[...], hn, precision=HIGHEST) + br_ref[:, 0:1]
        vals, onehots, idxs = [], [], []
        l = logits
        for _ in range(TOP_K):
            mx = jnp.max(l, axis=0, keepdims=True)
            first = jnp.min(jnp.where(l == mx, eidx, 1e9), axis=0, keepdims=True)
            pick = eidx == first
            vals.append(mx)
            idxs.append(first)
            onehots.append(pick.astype(F32))
            l = jnp.where(pick, -jnp.inf, l)
        exps = [jnp.exp(v - vals[0]) for v in vals]
        inv = 1.0 / (exps[0] + exps[1] + exps[2] + exps[3])
        total = onehots[0] + onehots[1] + onehots[2] + onehots[3]
        cum = _dot(total.astype(BF16), incl)
        excl = cum - total + seen
        ranks = [jnp.sum(oh * excl, axis=0, keepdims=True) for oh in onehots]
        seen = seen + jnp.sum(total, axis=-1, keepdims=True)

        route = jnp.zeros((8, tm), F32)
        for k in range(TOP_K):
            route = jnp.where(row8 == k, idxs[k], route)
            route = jnp.where(row8 == TOP_K + k, ranks[k], route)
        route_ref[:, rows] = route.astype(jnp.int32)
        wt = jnp.zeros((LANES, tm), F32)
        for k in range(TOP_K):
            wt = jnp.where(rowl == k, exps[k] * inv, wt)
        wts_ref[rows, :] = wt.T

    carry[...] = jnp.broadcast_to(seen, carry.shape)
    counts_ref[...] = carry[...].astype(jnp.int32)


def _out_proj(x2, h_m, h_b, w_out, g, w_router_t, b_router):
    T, D = x2.shape
    tm = TOK_TILE * OUT_SUBTILES
    W = GROUP_WIDTH
    return pl.pallas_call(
        _out_proj_kernel,
        grid=(T // tm,),
        in_specs=[
            pl.BlockSpec((tm, D), lambda i: (i, 0)),
            pl.BlockSpec((tm, W), lambda i: (i, 0)),
            pl.BlockSpec((tm, W), lambda i: (i, 0)),
            pl.BlockSpec((2 * W, D), lambda i: (0, 0)),
            pl.BlockSpec((1, D), lambda i: (0, 0)),
            pl.BlockSpec((N_EXPERTS, D), lambda i: (0, 0)),
            pl.BlockSpec((N_EXPERTS, LANES), lambda i: (0, 0)),
        ],
        out_specs=[
            pl.BlockSpec((tm, D), lambda i: (i, 0)),
            pl.BlockSpec((tm * TOKEN_SUBLANES, LANES), lambda i: (i, 0)),
            pl.BlockSpec((8, tm), lambda i: (0, i)),
            pl.BlockSpec((tm, LANES), lambda i: (i, 0)),
            pl.BlockSpec((N_EXPERTS, LANES), lambda i: (0, 0)),
        ],
        out_shape=[
            jax.ShapeDtypeStruct((T, D), F32),
            jax.ShapeDtypeStruct((T * TOKEN_SUBLANES, LANES), F32),
            jax.ShapeDtypeStruct((8, T), jnp.int32),
            jax.ShapeDtypeStruct((T, LANES), F32),
            jax.ShapeDtypeStruct((N_EXPERTS, LANES), jnp.int32),
        ],
        scratch_shapes=[pltpu.VMEM((N_EXPERTS, LANES), F32)],
        compiler_params=_cparams(("arbitrary",)),
        name="out_proj",
    )(x2, h_m, h_b, w_out, g, w_router_t, b_router)


TOKEN_SUBLANES = 8


def _store_token_tiles(ref, val, row0=0):
    n, d = val.shape
    assert d == TOKEN_SUBLANES * LANES
    for s in range(TOKEN_SUBLANES):
        ref[pl.ds(row0 * TOKEN_SUBLANES + s, n, stride=TOKEN_SUBLANES), :] = val[:, s * LANES:(s + 1) * LANES]


def _load_token_tiles(ref):
    n = ref.shape[0] // TOKEN_SUBLANES
    return jnp.concatenate([ref[pl.ds(s, n, stride=TOKEN_SUBLANES), :] for s in range(TOKEN_SUBLANES)], axis=1)


def _token_tile(ref, row):
    return ref.at[pl.ds(pl.multiple_of(row * TOKEN_SUBLANES, TOKEN_SUBLANES), TOKEN_SUBLANES), :]


def _row_copy(src, s_row, dst, d_row, sem):
    return pltpu.make_async_copy(_token_tile(src, s_row), _token_tile(dst, d_row), sem)


def _dispatch_kernel(ends_ref, padded_ref, hn_ref, pos_hbm, xs_hbm, zbuf, pos_smem, pos_sem, row_sem, z_sem):
    tm = TOK_TILE
    te = EXPERT_TILE
    step = pl.program_id(0)

    @pl.when(step == 0)
    def _():
        zbuf[...] = jnp.zeros_like(zbuf)
        total = ends_ref[N_EXPERTS - 1]
        n_rows = xs_hbm.shape[0] // TOKEN_SUBLANES

        def zero_tile(start):
            rows = te * TOKEN_SUBLANES
            return pltpu.make_async_copy(
                zbuf, xs_hbm.at[pl.ds(pl.multiple_of(start * TOKEN_SUBLANES, rows), rows), :], z_sem)

        for e in range(N_EXPERTS):
            @pl.when(padded_ref[e] > 0)
            def _():
                zero_tile(ends_ref[e] - te).start()

            @pl.when(total + e * te < n_rows)
            def _():
                zero_tile(total + e * te).start()
        for e in range(N_EXPERTS):
            @pl.when(padded_ref[e] > 0)
            def _():
                zero_tile(0).wait()

            @pl.when(total + e * te < n_rows)
            def _():
                zero_tile(0).wait()

    cp = pltpu.make_async_copy(pos_hbm.at[step], pos_smem, pos_sem)
    cp.start()
    cp.wait()

    def issue(t, _):
        for k in range(TOP_K):
            _row_copy(hn_ref, t, xs_hbm, pos_smem[k * tm + t], row_sem).start(priority=k % 2)
        return 0

    lax.fori_loop(0, tm, issue, 0)

    def drain(t, _):
        for k in range(TOP_K):
            _row_copy(hn_ref, 0, xs_hbm, 0, row_sem).wait()
        return 0

    lax.fori_loop(0, tm, drain, 0)


def _dispatch(hn, pos_tiles, ends, padded, n_rows):
    tm = TOK_TILE
    T = hn.shape[0] // TOKEN_SUBLANES
    grid_spec = pltpu.PrefetchScalarGridSpec(
        num_scalar_prefetch=2,
        grid=(T // tm,),
        in_specs=[
            pl.BlockSpec((tm * TOKEN_SUBLANES, LANES), lambda i, e, p: (i, 0)),
            pl.BlockSpec(memory_space=pl.ANY),
        ],
        out_specs=pl.BlockSpec(memory_space=pl.ANY),
        scratch_shapes=[
            pltpu.VMEM((EXPERT_TILE * TOKEN_SUBLANES, LANES), F32),
            pltpu.SMEM((TOP_K * tm,), jnp.int32),
            pltpu.SemaphoreType.DMA,
            pltpu.SemaphoreType.DMA,
            pltpu.SemaphoreType.DMA,
        ],
    )
    return pl.pallas_call(
        _dispatch_kernel,
        grid_spec=grid_spec,
        out_shape=jax.ShapeDtypeStruct((n_rows * TOKEN_SUBLANES, LANES), F32),
        compiler_params=_cparams(("arbitrary",)),
        name="dispatch",
    )(ends, padded, hn, pos_tiles)


def _experts_kernel(te_ref, nu_ref, x_ref, wgu_ref, bgu_ref, wdn_ref, bdn_ref, y_ref, wgu_b, wdn_b):
    i = pl.program_id(0)
    F = wdn_ref.shape[1]
    prev = te_ref[jnp.maximum(i - 1, 0)]
    fresh = (i == 0) | (te_ref[i] != prev)

    @pl.when(fresh & (i < nu_ref[0]))
    def _():
        wgu_b[...] = wgu_ref[0].astype(BF16)
        wdn_b[...] = wdn_ref[0].astype(BF16)

    @pl.when(i < nu_ref[0])
    def _():
        xb = _load_token_tiles(x_ref).astype(BF16)
        hu = _dot(xb, wgu_b[...]) + bgu_ref[0]
        glu = jnp.minimum(hu[:, :F], SWIGLU_LIMIT)
        lin = jnp.clip(hu[:, F:], -SWIGLU_LIMIT, SWIGLU_LIMIT)
        act = glu * _sigmoid(SWIGLU_ALPHA * glu) * (lin + 1.0)
        _store_token_tiles(y_ref, _dot(act.astype(BF16), wdn_b[...]) + bdn_ref[0])

    @pl.when(i >= nu_ref[0])
    def _():
        y_ref[...] = jnp.zeros_like(y_ref)


def _experts(tile_expert, n_used, x_sorted, w_gu, b_gu, w_dn, b_dn):
    P = x_sorted.shape[0] // TOKEN_SUBLANES
    E, D, F2 = w_gu.shape
    F = F2 // 2
    tm = EXPERT_TILE
    n_tiles = P // tm

    def row_map(i, te, nu):
        return (jnp.maximum(jnp.minimum(i, nu[0] - 1), 0), 0)

    def exp_map(i, te, nu):
        return (te[i], 0, 0)

    grid_spec = pltpu.PrefetchScalarGridSpec(
        num_scalar_prefetch=2,
        grid=(n_tiles,),
        in_specs=[
            pl.BlockSpec((tm * TOKEN_SUBLANES, LANES), row_map),
            pl.BlockSpec((1, D, F2), exp_map),
            pl.BlockSpec((1, 1, F2), exp_map),
            pl.BlockSpec((1, F, D), exp_map),
            pl.BlockSpec((1, 1, D), exp_map),
        ],
        out_specs=pl.BlockSpec((tm * TOKEN_SUBLANES, LANES), lambda i, te, nu: (i, 0)),
        scratch_shapes=[pltpu.VMEM((D, F2), BF16), pltpu.VMEM((F, D), BF16)],
    )
    return pl.pallas_call(
        _experts_kernel,
        grid_spec=grid_spec,
        out_shape=jax.ShapeDtypeStruct((P * TOKEN_SUBLANES, LANES), F32),
        compiler_params=_cparams(("arbitrary",)),
        name="experts",
    )(tile_expert, n_used, x_sorted, w_gu, b_gu.reshape(E, 1, F2), w_dn, b_dn.reshape(E, 1, D))


def _combine_kernel(x1_ref, wts_ref, p_ref, g_ref, wg_ref, wp_ref, pos_hbm, ys_hbm,
                    out_ref, ybuf, pos_smem, pos_sem, row_sem):
    tm = TOK_TILE
    step = pl.program_id(0)
    cp = pltpu.make_async_copy(pos_hbm.at[step], pos_smem, pos_sem)
    cp.start()
    cp.wait()

    def issue(t, _):
        for k in range(TOP_K):
            _row_copy(ys_hbm, pos_smem[k * tm + t], ybuf.at[k], t, row_sem).start(priority=k % 2)
        return 0

    lax.fori_loop(0, tm, issue, 0)

    def drain(t, _):
        for k in range(TOP_K):
            _row_copy(ys_hbm, 0, ybuf.at[k], 0, row_sem).wait()
        return 0

    lax.fori_loop(0, tm, drain, 0)

    w = wts_ref[...]
    x2 = x1_ref[...]
    for k in range(TOP_K):
        x2 = x2 + w[:, k:k + 1] * _load_token_tiles(ybuf.at[k])
    hn = x2 * lax.rsqrt(jnp.mean(x2 * x2, axis=-1, keepdims=True) + RMS_EPS) * g_ref[...]
    gate = _sigmoid(_dot(hn.astype(BF16), wg_ref[...]))
    out_ref[...] = x2 + gate * _dot(p_ref[...].astype(BF16), wp_ref[...])


def _combine(x1, wts, p2, g, w_gate, w_proj, pos_tiles, y_sorted):
    T, D = x1.shape
    tm = TOK_TILE
    PD = p2.shape[1]
    return pl.pallas_call(
        _combine_kernel,
        grid=(T // tm,),
        in_specs=[
            pl.BlockSpec((tm, D), lambda i: (i, 0)),
            pl.BlockSpec((tm, LANES), lambda i: (i, 0)),
            pl.BlockSpec((tm, PD), lambda i: (i, 0)),
            pl.BlockSpec((1, D), lambda i: (0, 0)),
            pl.BlockSpec((D, D), lambda i: (0, 0)),
            pl.BlockSpec((PD, D), lambda i: (0, 0)),
            pl.BlockSpec(memory_space=pl.ANY),
            pl.BlockSpec(memory_space=pl.ANY),
        ],
        out_specs=pl.BlockSpec((tm, D), lambda i: (i, 0)),
        out_shape=jax.ShapeDtypeStruct((T, D), F32),
        scratch_shapes=[
            pltpu.VMEM((TOP_K, tm * TOKEN_SUBLANES, LANES), F32),
            pltpu.SMEM((TOP_K * tm,), jnp.int32),
            pltpu.SemaphoreType.DMA,
            pltpu.SemaphoreType.DMA,
        ],
        compiler_params=_cparams(("arbitrary",)),
        name="combine",
    )(x1, wts, p2, g, w_gate, w_proj, pos_tiles, y_sorted)


def _routing_tables(route, counts, n_tokens):
    te = EXPERT_TILE
    tm = TOK_TILE
    idx = route[:TOP_K]
    rank = route[TOP_K:]
    cnt = counts[:, 0]
    padded = ((cnt + te - 1) // te) * te
    ends = jnp.cumsum(padded)
    offs = ends - padded
    e_ids = jnp.arange(N_EXPERTS, dtype=jnp.int32)[:, None, None]
    pos = jnp.sum(jnp.where(idx[None] == e_ids, offs[:, None, None], 0), axis=0) + rank
    pos_tiles = pos.reshape(TOP_K, n_tokens // tm, tm).transpose(1, 0, 2).reshape(n_tokens // tm, TOP_K * tm)
    n_rows = n_tokens * TOP_K + N_EXPERTS * te
    starts = jnp.arange(n_rows // te, dtype=jnp.int32) * te
    tile_expert = jnp.minimum(jnp.sum((starts[:, None] >= ends[None, :]).astype(jnp.int32), axis=1), N_EXPERTS - 1)
    n_used = (ends[-1] // te).astype(jnp.int32).reshape(1)
    return (pos_tiles.astype(jnp.int32), tile_expert.astype(jnp.int32), n_used,
            ends.astype(jnp.int32), padded.astype(jnp.int32), n_rows)


def kernel(x, p, mix_norm_g, w_in, conv_w, conv_b, igate_b, fgate_b, mlstm_head_g, q_norm_g, k_norm_g, w_out, ffn_norm_g, w_router, b_router, w_gate_up, b_gate_up, w_down, b_down, ple_norm_g, w_ple_gate, w_ple_proj):
    B, S, D = x.shape
    T = B * S
    depth = mix_norm_g.shape[0]
    for l in range(depth):
        h_m, h_b = _mixing(x, mix_norm_g[l], w_in[l], conv_w[l], conv_b[l], igate_b[l], fgate_b[l],
                           mlstm_head_g[l], q_norm_g[l], k_norm_g[l])
        b_r = jnp.broadcast_to(b_router[l].reshape(N_EXPERTS, 1), (N_EXPERTS, LANES))
        x1, hn, route, wts, counts = _out_proj(
            x.reshape(T, D), h_m.reshape(T, GROUP_WIDTH), h_b.reshape(T, GROUP_WIDTH),
            w_out[l].astype(BF16), ffn_norm_g[l].reshape(1, D), w_router[l].T, b_r)
        pos_tiles, tile_expert, n_used, ends, padded, n_rows = _routing_tables(route, counts, T)
        x_sorted = _dispatch(hn, pos_tiles, ends, padded, n_rows)
        y_sorted = _experts(tile_expert, n_used, x_sorted, w_gate_up[l], b_gate_up[l], w_down[l], b_down[l])
        out = _combine(x1, wts, p[l].reshape(T, -1), ple_norm_g[l].reshape(1, D),
                       w_ple_gate[l].astype(BF16), w_ple_proj[l].astype(BF16), pos_tiles, y_sorted)
        x = out.reshape(B, S, D)
    return x
```

```python
import functools

import jax
import jax.numpy as jnp
from jax import lax
from jax.experimental import pallas as pl
from jax.experimental.pallas import tpu as pltpu

F32 = jnp.float32
BF16 = jnp.bfloat16
HIGHEST = lax.Precision.HIGHEST

HEAD_DIM = 64
LANES = 128
N_HEADS = 8
GROUP_WIDTH = N_HEADS * HEAD_DIM
N_PAIRS = N_HEADS // 2
MLSTM_CHUNK = 128
CONV_WIDTH = 4
MOBA_BLOCK = 256
MOBA_TOPK = 3
N_EXPERTS = 32
TOP_K = 4
SWIGLU_LIMIT = 7.0
SWIGLU_ALPHA = 1.702
RMS_EPS = 1e-6
NEG = -1e30
VMEM_LIMIT = 56 * 1024 * 1024

COL_QK = 0
COL_V = 2 * GROUP_WIDTH
COL_O = 3 * GROUP_WIDTH
COL_MQ = 4 * GROUP_WIDTH
COL_MK = 5 * GROUP_WIDTH
COL_MV = 6 * GROUP_WIDTH
MAIN_COLS = 7 * GROUP_WIDTH


def _cparams(sem):
    return pltpu.CompilerParams(dimension_semantics=sem, vmem_limit_bytes=VMEM_LIMIT)


def _dot(a, b, **kw):
    return jnp.dot(a, b, preferred_element_type=F32, **kw)


def _dot_nt(a, b, **kw):
    return lax.dot_general(a, b, (((1,), (1,)), ((), ())), preferred_element_type=F32, **kw)


def _split3(x):
    hi = x.astype(BF16)
    r = x - hi.astype(F32)
    mid = r.astype(BF16)
    lo = (r - mid.astype(F32)).astype(BF16)
    return hi, mid, lo


def _dot_rhs01(a, b01):
    b = b01.astype(BF16)
    hi, mid, lo = _split3(a)
    return _dot(hi, b) + _dot(mid, b) + _dot(lo, b)


def _dot_lhs01(a01, b):
    a = a01.astype(BF16)
    hi, mid, lo = _split3(b)
    return _dot(a, hi) + _dot(a, mid) + _dot(a, lo)


def _sigmoid(x):
    return 1.0 / (1.0 + jnp.exp(-x))


def _lane_half_mask(shape):
    return lax.broadcasted_iota(jnp.int32, shape, len(shape) - 1) < HEAD_DIM


def _pair_block_ones():
    r = lax.broadcasted_iota(jnp.int32, (LANES, LANES), 0) // HEAD_DIM
    c = lax.broadcasted_iota(jnp.int32, (LANES, LANES), 1) // HEAD_DIM
    return r == c


def _in_proj_kernel(x_ref, g_ref, w_ref, wg_ref, u_ref, gates_ref, *, n_chunk):
    x = x_ref[...]
    h = x * lax.rsqrt(jnp.mean(x * x, axis=-1, keepdims=True) + RMS_EPS) * g_ref[...]
    hb = h.astype(BF16)
    for c in range(MAIN_COLS // n_chunk):
        u_ref[:, c * n_chunk:(c + 1) * n_chunk] = _dot(hb, w_ref[:, c * n_chunk:(c + 1) * n_chunk])
    gates_ref[...] = _dot(hb, wg_ref[...])


def _in_proj(x2, g, w_main, w_gates, tm=512):
    T, D = x2.shape
    return pl.pallas_call(
        functools.partial(_in_proj_kernel, n_chunk=GROUP_WIDTH),
        grid=(T // tm,),
        in_specs=[
            pl.BlockSpec((tm, D), lambda i: (i, 0)),
            pl.BlockSpec((1, D), lambda i: (0, 0)),
            pl.BlockSpec((D, MAIN_COLS), lambda i: (0, 0)),
            pl.BlockSpec((D, LANES), lambda i: (0, 0)),
        ],
        out_specs=[
            pl.BlockSpec((tm, MAIN_COLS), lambda i: (i, 0)),
            pl.BlockSpec((tm, LANES), lambda i: (i, 0)),
        ],
        out_shape=[
            jax.ShapeDtypeStruct((T, MAIN_COLS), F32),
            jax.ShapeDtypeStruct((T, LANES), F32),
        ],
        compiler_params=_cparams(("parallel",)),
        name="in_proj",
    )(x2, g, w_main, w_gates)


def _mlstm_kernel(qk_ref, v_ref, o_ref, gates_ref, cw_ref, cb_ref, gb_ref, hg_ref,
                  out_ref, ext_ref, c_state, n_state, m_state):
    L = MLSTM_CHUNK
    W = GROUP_WIDTH
    chunk = pl.program_id(1)

    @pl.when(chunk == 0)
    def _():
        ext_ref[0:8, :] = jnp.zeros((8, 2 * W), F32)
        c_state[...] = jnp.zeros_like(c_state)
        n_state[...] = jnp.zeros_like(n_state)
        m_state[...] = jnp.zeros_like(m_state)

    ext_ref[8:8 + L, :] = qk_ref[0]
    y = cb_ref[...] + ext_ref[8:8 + L, :] * cw_ref[3:4, :]
    for j in range(1, CONV_WIDTH):
        y = y + ext_ref[8 - j:8 - j + L, :] * cw_ref[3 - j:4 - j, :]
    ext_ref[0:8, :] = ext_ref[L:L + 8, :]
    a = y * _sigmoid(y)
    q_all = a[:, :W] * (HEAD_DIM ** -0.5)
    k_all = a[:, W:]

    pre = gates_ref[0] + gb_ref[...]
    lane = lax.broadcasted_iota(jnp.int32, (L, LANES), 1)
    logsig = jnp.minimum(pre, 0.0) - jnp.log(1.0 + jnp.exp(-jnp.abs(pre)))
    G = jnp.where(lane < N_HEADS, pre, logsig)
    row = lax.broadcasted_iota(jnp.int32, (L, L), 0)
    col = lax.broadcasted_iota(jnp.int32, (L, L), 1)
    causal = col <= row
    Bc = _dot_lhs01(causal, G)
    GT = G.T
    BT = Bc.T

    first_half = _lane_half_mask((L, LANES))
    first_half_row = _lane_half_mask((1, LANES))
    blockdiag = _pair_block_ones()

    for p in range(N_PAIRS):
        sl = slice(p * LANES, (p + 1) * LANES)
        q_pair = q_all[:, sl]
        k_pair = k_all[:, sl]
        v_pair = v_ref[0][:, sl]
        qb = q_pair.astype(BF16)
        kb = k_pair.astype(BF16)
        vb = v_pair.astype(BF16)
        Cp = c_state[p]
        Np = n_state[p]
        q_c = _dot(qb, Cp.astype(BF16))
        q_n = _dot(qb, Np.astype(BF16))
        h_halves = []
        w_cols = []
        decays = []
        for hh in range(2):
            h = 2 * p + hh
            bcol = Bc[:, N_HEADS + h:N_HEADS + h + 1]
            brow = BT[N_HEADS + h:N_HEADS + h + 1, :]
            irow = GT[h:h + 1, :]
            icol = G[:, h:h + 1]
            m_prev = m_state[h:h + 1, 0:1]
            logD = jnp.where(causal, bcol - brow + irow, -jnp.inf)
            inter = bcol + m_prev
            m_t = jnp.maximum(inter, jnp.max(logD, axis=-1, keepdims=True))
            Dm = jnp.exp(logD - m_t)
            sc = jnp.exp(inter - m_t)
            in_head = first_half if hh == 0 else jnp.logical_not(first_half)
            q_h = jnp.where(in_head, q_pair, 0.0).astype(BF16)
            s = _dot_nt(q_h, kb) * Dm
            num = _dot(s.astype(BF16), vb) + sc * q_c
            den = jnp.sum(s, axis=-1, keepdims=True) + sc * q_n
            h_halves.append(num / jnp.maximum(jnp.abs(den), jnp.exp(-m_t)))
            bL = bcol[L - 1:L, :]
            logw = bL - bcol + icol
            m_new = jnp.maximum(bL + m_prev, jnp.max(logw, axis=0, keepdims=True))
            w_cols.append(jnp.exp(logw - m_new))
            decays.append(jnp.exp(bL + m_prev - m_new))
            m_state[h:h + 1, :] = jnp.broadcast_to(m_new, (1, LANES))
        h_pair = jnp.where(first_half, h_halves[0], h_halves[1])
        wk = jnp.where(first_half, w_cols[0], w_cols[1]) * k_pair
        wkT = wk.T
        dec = jnp.where(first_half_row, decays[0], decays[1])
        c_state[p] = dec * Cp + jnp.where(blockdiag, _dot(wkT.astype(BF16), vb), 0.0)
        n_state[p] = dec * Np + jnp.where(blockdiag, jnp.sum(wkT, axis=-1, keepdims=True), 0.0)
        ms = _dot_rhs01(h_pair * h_pair, blockdiag) * (1.0 / HEAD_DIM)
        hn = h_pair * lax.rsqrt(ms + RMS_EPS) * hg_ref[:, sl]
        out_ref[0, :, sl] = (hn * _sigmoid(o_ref[0][:, sl])).astype(out_ref.dtype)


def _mlstm(u3, gates3, conv_w, conv_b, gate_b, head_g):
    B, S, _ = u3.shape
    L = MLSTM_CHUNK
    W = GROUP_WIDTH
    return pl.pallas_call(
        _mlstm_kernel,
        grid=(B, S // L),
        in_specs=[
            pl.BlockSpec((1, L, 2 * W), lambda b, c: (b, c, COL_QK // (2 * W))),
            pl.BlockSpec((1, L, W), lambda b, c: (b, c, COL_V // W)),
            pl.BlockSpec((1, L, W), lambda b, c: (b, c, COL_O // W)),
            pl.BlockSpec((1, L, LANES), lambda b, c: (b, c, 0)),
            pl.BlockSpec((CONV_WIDTH, 2 * W), lambda b, c: (0, 0)),
            pl.BlockSpec((1, 2 * W), lambda b, c: (0, 0)),
            pl.BlockSpec((1, LANES), lambda b, c: (0, 0)),
            pl.BlockSpec((1, W), lambda b, c: (0, 0)),
        ],
        out_specs=pl.BlockSpec((1, L, W), lambda b, c: (b, c, 0)),
        out_shape=jax.ShapeDtypeStruct((B, S, W), BF16),
        scratch_shapes=[
            pltpu.VMEM((L + 8, 2 * W), F32),
            pltpu.VMEM((N_PAIRS, LANES, LANES), F32),
            pltpu.VMEM((N_PAIRS, LANES, LANES), F32),
            pltpu.VMEM((N_HEADS, LANES), F32),
        ],
        compiler_params=_cparams(("parallel", "arbitrary")),
        name="mlstm",
    )(u3, u3, u3, gates3, conv_w, conv_b, gate_b, head_g)


AUX_GATE = 8


def _moba_kernel(q_ref, k_ref, v_ref, qg_ref, kg_ref, out_ref, kaug, vaug_t, kmean, s_buf_a, s_buf_b,
                 *, n_blocks):
    BS = MOBA_BLOCK
    HD = HEAD_DIM
    pair = pl.program_id(1)
    own = pl.program_id(2)

    lane = lax.broadcasted_iota(jnp.int32, (BS, LANES), 1)
    rowi = lax.broadcasted_iota(jnp.int32, (BS, LANES), 0)

    @pl.when(own == 0)
    def _():
        kmean[...] = jnp.zeros_like(kmean)
        for hh in range(2):
            flag_lane = (HD if hh == 0 else 0) + AUX_GATE + n_blocks
            kaug[hh, n_blocks] = jnp.where(lane == flag_lane, 1.0, 0.0).astype(BF16)
            vaug_t[hh, n_blocks] = jnp.zeros((LANES, BS), BF16)

    k_pair = k_ref[0]
    k_ms = _dot_rhs01(k_pair * k_pair, _pair_block_ones()) * (1.0 / HD)
    kn = k_pair * lax.rsqrt(k_ms + RMS_EPS) * kg_ref[...]
    kmean[pl.ds(own, 1), :] = jnp.mean(kn, axis=0, keepdims=True)

    q_t = q_ref[0].T
    v_t = v_ref[0].T
    row_t = lax.broadcasted_iota(jnp.int32, (LANES, BS), 0)
    blk = lax.broadcasted_iota(jnp.int32, (n_blocks, BS), 0)
    blk_f = blk.astype(F32)
    km = kmean[...]
    lane_km = lax.broadcasted_iota(jnp.int32, (n_blocks, LANES), 1)

    q_aug_t, k_blk, v_blk_t = [], [], []
    for hh in range(2):
        head = 2 * pair + hh
        slope = jnp.exp2(-jnp.full((1, 1), 1.0, F32) * (head + 1).astype(F32))
        base = HD if hh == 0 else 0
        hs = slice(hh * HD, (hh + 1) * HD)
        q_h = q_t[hs, :]
        q_ms = jnp.mean(q_h * q_h, axis=0, keepdims=True)
        qn_h = q_h * lax.rsqrt(q_ms + RMS_EPS) * qg_ref[hs, :] * (HD ** -0.5)

        km_h = jnp.where((lane_km >= hh * HD) & (lane_km < (hh + 1) * HD), km, 0.0)
        zeros_h = jnp.zeros((HD, BS), F32)
        qn_full = jnp.concatenate([qn_h, zeros_h] if hh == 0 else [zeros_h, qn_h], axis=0)
        gate = _dot(km_h, qn_full, precision=HIGHEST)
        g = jnp.where(blk < own, gate, -jnp.inf)
        sel = blk == own
        for _ in range(MOBA_TOPK):
            mx = jnp.max(g, axis=0, keepdims=True)
            first = jnp.min(jnp.where(g == mx, blk_f, 1e9), axis=0, keepdims=True)
            pick = (blk_f == first) & (mx > -jnp.inf)
            sel = sel | pick
            g = jnp.where(pick, -jnp.inf, g)
        bias = jnp.where(sel, 0.0, NEG)
        r8 = lax.broadcasted_iota(jnp.int32, (AUX_GATE, BS), 0)
        ones2 = jnp.where(r8 < 2, 1.0, 0.0)
        pad_flag = jnp.where(r8 == 0, NEG, 0.0)
        aux = jnp.concatenate(
            [ones2, bias, pad_flag, jnp.zeros((HD - 2 * AUX_GATE - n_blocks, BS), F32)], axis=0)
        parts = [qn_h, aux] if hh == 0 else [aux, qn_h]
        q_aug_t.append(jnp.concatenate(parts, axis=0).astype(BF16))

        in_head = (lane >= hh * HD) & (lane < (hh + 1) * HD)
        k_aux = jnp.where(lane == base, slope * rowi.astype(F32),
                          jnp.where(lane == base + 1, slope * (own * BS).astype(F32),
                                    jnp.where(lane - (base + AUX_GATE) == own, 1.0, 0.0)))
        k_own = jnp.where(in_head, kn, k_aux).astype(BF16)
        in_head_t = (row_t >= hh * HD) & (row_t < (hh + 1) * HD)
        v_own_t = jnp.where(in_head_t, v_t, jnp.where(row_t == base, 1.0, 0.0)).astype(BF16)
        kaug[hh, own] = k_own
        vaug_t[hh, own] = v_own_t
        k_blk.append(k_own)
        v_blk_t.append(v_own_t)

    def scores(hh, j):
        return _dot(kaug[hh, j], q_aug_t[hh])

    def update(hh, j, s, m, acc):
        m_new = jnp.maximum(m, jnp.max(s, axis=0, keepdims=True))
        p = jnp.exp(s - m_new)
        return m_new, jnp.exp(m - m_new) * acc + _dot(vaug_t[hh, j], p.astype(BF16))

    def block_of(t, u):
        idx = 2 * t + u
        return jnp.where(idx == 0, own, jnp.where(idx <= own, idx - 1, n_blocks))

    def score_pair(buf, t):
        for u in range(2):
            for hh in range(2):
                buf[u, hh] = scores(hh, block_of(t, u))

    def update_pair(buf, t, carry):
        carry = list(carry)
        for u in range(2):
            for hh in range(2):
                carry[2 * hh], carry[2 * hh + 1] = update(
                    hh, block_of(t, u), buf[u, hh], carry[2 * hh], carry[2 * hh + 1])
        return tuple(carry)

    n_pairs = (own + 2) // 2
    n_quads = n_pairs // 2

    r2 = lax.broadcasted_iota(jnp.int32, (BS, BS), 0)
    c2 = lax.broadcasted_iota(jnp.int32, (BS, BS), 1)
    causal_t = r2 <= c2
    for hh in range(2):
        s_buf_a[0, hh] = jnp.where(causal_t, _dot(k_blk[hh], q_aug_t[hh]), NEG)
        s_buf_a[1, hh] = scores(hh, block_of(0, 1))
    carry = []
    for hh in range(2):
        carry += [jnp.full((1, BS), NEG, F32), jnp.zeros((LANES, BS), F32)]

    def body4(t, carry):
        score_pair(s_buf_b, 2 * t + 1)
        carry = update_pair(s_buf_a, 2 * t, carry)
        score_pair(s_buf_a, jnp.minimum(2 * t + 2, n_pairs - 1))
        return update_pair(s_buf_b, 2 * t + 1, carry)

    def body2(t, carry):
        return update_pair(s_buf_a, t, carry)

    carry = lax.fori_loop(0, n_quads, body4, tuple(carry))
    carry = lax.fori_loop(2 * n_quads, n_pairs, body2, carry)

    outs = []
    for hh in range(2):
        base = HD if hh == 0 else 0
        acc = carry[2 * hh + 1]
        outs.append(acc / acc[base:base + 1, :])
    out_t = jnp.where(row_t < HD, outs[0], outs[1])
    out_ref[0] = out_t.T.astype(out_ref.dtype)


def _moba(u3, q_g_t, k_g):
    B, S, _ = u3.shape
    BS = MOBA_BLOCK
    nb = S // BS
    assert 2 * AUX_GATE + nb <= HEAD_DIM and nb % 8 == 0
    return pl.pallas_call(
        functools.partial(_moba_kernel, n_blocks=nb),
        grid=(B, N_PAIRS, nb),
        in_specs=[
            pl.BlockSpec((1, BS, LANES), lambda b, p, i: (b, i, COL_MQ // LANES + p)),
            pl.BlockSpec((1, BS, LANES), lambda b, p, i: (b, i, COL_MK // LANES + p)),
            pl.BlockSpec((1, BS, LANES), lambda b, p, i: (b, i, COL_MV // LANES + p)),
            pl.BlockSpec((LANES, BS), lambda b, p, i: (0, 0)),
            pl.BlockSpec((1, LANES), lambda b, p, i: (0, 0)),
        ],
        out_specs=pl.BlockSpec((1, BS, LANES), lambda b, p, i: (b, i, p)),
        out_shape=jax.ShapeDtypeStruct((B, S, GROUP_WIDTH), BF16),
        scratch_shapes=[
            pltpu.VMEM((2, nb + 1, BS, LANES), BF16),
            pltpu.VMEM((2, nb + 1, LANES, BS), BF16),
            pltpu.VMEM((nb, LANES), F32),
            pltpu.VMEM((2, 2, BS, BS), F32),
            pltpu.VMEM((2, 2, BS, BS), F32),
        ],
        compiler_params=_cparams(("parallel", "parallel", "arbitrary")),
        name="moba",
    )(u3, u3, u3, q_g_t, k_g)


def _mixing(x, mix_norm_g, w_in, conv_w, conv_b, igate_b, fgate_b, mlstm_head_g, q_norm_g, k_norm_g):
    B, S, D = x.shape
    W = GROUP_WIDTH
    n_gate = 2 * N_HEADS
    w_main = jnp.concatenate([w_in[:, :4 * W], w_in[:, 4 * W + n_gate:]], axis=1).astype(BF16)
    w_gates = jnp.pad(w_in[:, 4 * W:4 * W + n_gate], ((0, 0), (0, LANES - n_gate))).astype(BF16)
    u, gates = _in_proj(x.reshape(B * S, D), mix_norm_g.reshape(1, D), w_main, w_gates)
    u3 = u.reshape(B, S, MAIN_COLS)
    gates3 = gates.reshape(B, S, LANES)
    gate_b = jnp.pad(jnp.concatenate([igate_b, fgate_b]), (0, LANES - n_gate)).reshape(1, LANES)
    h_m = _mlstm(u3, gates3, conv_w, conv_b.reshape(1, 2 * W), gate_b, mlstm_head_g.reshape(1, W))
    qg_t = jnp.broadcast_to(jnp.concatenate([q_norm_g, q_norm_g]).reshape(LANES, 1), (LANES, MOBA_BLOCK))
    kg2 = jnp.concatenate([k_norm_g, k_norm_g]).reshape(1, LANES)
    h_b = _moba(u3, qg_t, kg2)
    return h_m, h_b


TOK_TILE = 256
EXPERT_TILE = 256
OUT_SUBTILES = 1


def _out_proj_kernel(x_ref, hm_ref, hb_ref, wo_ref, g_ref, wr_ref, br_ref,
                     x1_ref, hn_ref, route_ref, wts_ref, counts_ref, carry):
    tm = TOK_TILE
    W = GROUP_WIDTH
    step = pl.program_id(0)

    @pl.when(step == 0)
    def _():
        carry[...] = jnp.zeros_like(carry)

    eidx = lax.broadcasted_iota(jnp.int32, (N_EXPERTS, tm), 0).astype(F32)
    r2 = lax.broadcasted_iota(jnp.int32, (tm, tm), 0)
    c2 = lax.broadcasted_iota(jnp.int32, (tm, tm), 1)
    incl = (r2 <= c2).astype(BF16)
    row8 = lax.broadcasted_iota(jnp.int32, (8, tm), 0)
    rowl = lax.broadcasted_iota(jnp.int32, (LANES, tm), 0)
    seen = carry[:, 0:1]

    for sub in range(OUT_SUBTILES):
        rows = slice(sub * tm, (sub + 1) * tm)
        h_mix = jnp.concatenate([hm_ref[rows, :], hb_ref[rows, :]], axis=1)
        x1 = x_ref[rows, :] + _dot(h_mix, wo_ref[...])
        x1_ref[rows, :] = x1
        hn = x1 * lax.rsqrt(jnp.mean(x1 * x1, axis=-1, keepdims=True) + RMS_EPS) * g_ref[...]
        _store_token_tiles(hn_ref, hn, row0=sub * tm)

        logits = _dot_nt(wr_ref[...], hn, precision=HIGHEST) + br_ref[:, 0:1]
        vals, onehots, idxs = [], [], []
        l = logits
        for _ in range(TOP_K):
            mx = jnp.max(l, axis=0, keepdims=True)
            first = jnp.min(jnp.where(l == mx, eidx, 1e9), axis=0, keepdims=True)
            pick = eidx == first
            vals.append(mx)
            idxs.append(first)
            onehots.append(pick.astype(F32))
            l = jnp.where(pick, -jnp.inf, l)
        exps = [jnp.exp(v - vals[0]) for v in vals]
        inv = 1.0 / (exps[0] + exps[1] + exps[2] + exps[3])
        total = onehots[0] + onehots[1] + onehots[2] + onehots[3]
        cum = _dot(total.astype(BF16), incl)
        excl = cum - total + seen
        ranks = [jnp.sum(oh * excl, axis=0, keepdims=True) for oh in onehots]
        seen = seen + jnp.sum(total, axis=-1, keepdims=True)

        route = jnp.zeros((8, tm), F32)
        for k in range(TOP_K):
            route = jnp.where(row8 == k, idxs[k], route)
            route = jnp.where(row8 == TOP_K + k, ranks[k], route)
        route_ref[:, rows] = route.astype(jnp.int32)
        wt = jnp.zeros((LANES, tm), F32)
        for k in range(TOP_K):
            wt = jnp.where(rowl == k, exps[k] * inv, wt)
        wts_ref[rows, :] = wt.T

    carry[...] = jnp.broadcast_to(seen, carry.shape)
    counts_ref[...] = carry[...].astype(jnp.int32)


def _out_proj(x2, h_m, h_b, w_out, g, w_router_t, b_router):
    T, D = x2.shape
    tm = TOK_TILE * OUT_SUBTILES
    W = GROUP_WIDTH
    return pl.pallas_call(
        _out_proj_kernel,
        grid=(T // tm,),
        in_specs=[
            pl.BlockSpec((tm, D), lambda i: (i, 0)),
            pl.BlockSpec((tm, W), lambda i: (i, 0)),
            pl.BlockSpec((tm, W), lambda i: (i, 0)),
            pl.BlockSpec((2 * W, D), lambda i: (0, 0)),
            pl.BlockSpec((1, D), lambda i: (0, 0)),
            pl.BlockSpec((N_EXPERTS, D), lambda i: (0, 0)),
            pl.BlockSpec((N_EXPERTS, LANES), lambda i: (0, 0)),
        ],
        out_specs=[
            pl.BlockSpec((tm, D), lambda i: (i, 0)),
            pl.BlockSpec((tm * TOKEN_SUBLANES, LANES), lambda i: (i, 0)),
            pl.BlockSpec((8, tm), lambda i: (0, i)),
            pl.BlockSpec((tm, LANES), lambda i: (i, 0)),
            pl.BlockSpec((N_EXPERTS, LANES), lambda i: (0, 0)),
        ],
        out_shape=[
            jax.ShapeDtypeStruct((T, D), F32),
            jax.ShapeDtypeStruct((T * TOKEN_SUBLANES, LANES), F32),
            jax.ShapeDtypeStruct((8, T), jnp.int32),
            jax.ShapeDtypeStruct((T, LANES), F32),
            jax.ShapeDtypeStruct((N_EXPERTS, LANES), jnp.int32),
        ],
        scratch_shapes=[pltpu.VMEM((N_EXPERTS, LANES), F32)],
        compiler_params=_cparams(("arbitrary",)),
        name="out_proj",
    )(x2, h_m, h_b, w_out, g, w_router_t, b_router)


TOKEN_SUBLANES = 8


def _store_token_tiles(ref, val, row0=0):
    n, d = val.shape
    assert d == TOKEN_SUBLANES * LANES
    for s in range(TOKEN_SUBLANES):
        ref[pl.ds(row0 * TOKEN_SUBLANES + s, n, stride=TOKEN_SUBLANES), :] = val[:, s * LANES:(s + 1) * LANES]


def _load_token_tiles(ref):
    n = ref.shape[0] // TOKEN_SUBLANES
    return jnp.concatenate([ref[pl.ds(s, n, stride=TOKEN_SUBLANES), :] for s in range(TOKEN_SUBLANES)], axis=1)


def _token_tile(ref, row):
    return ref.at[pl.ds(pl.multiple_of(row * TOKEN_SUBLANES, TOKEN_SUBLANES), TOKEN_SUBLANES), :]


def _row_copy(src, s_row, dst, d_row, sem):
    return pltpu.make_async_copy(_token_tile(src, s_row), _token_tile(dst, d_row), sem)


def _dispatch_kernel(ends_ref, padded_ref, hn_ref, pos_hbm, xs_hbm, zbuf, pos_smem, pos_sem, row_sem, z_sem):
    tm = TOK_TILE
    te = EXPERT_TILE
    step = pl.program_id(0)

    @pl.when(step == 0)
    def _():
        zbuf[...] = jnp.zeros_like(zbuf)
        total = ends_ref[N_EXPERTS - 1]
        n_rows = xs_hbm.shape[0] // TOKEN_SUBLANES

        def zero_tile(start):
            rows = te * TOKEN_SUBLANES
            return pltpu.make_async_copy(
                zbuf, xs_hbm.at[pl.ds(pl.multiple_of(start * TOKEN_SUBLANES, rows), rows), :], z_sem)

        for e in range(N_EXPERTS):
            @pl.when(padded_ref[e] > 0)
            def _():
                zero_tile(ends_ref[e] - te).start()

            @pl.when(total + e * te < n_rows)
            def _():
                zero_tile(total + e * te).start()
        for e in range(N_EXPERTS):
            @pl.when(padded_ref[e] > 0)
            def _():
                zero_tile(0).wait()

            @pl.when(total + e * te < n_rows)
            def _():
                zero_tile(0).wait()

    cp = pltpu.make_async_copy(pos_hbm.at[step], pos_smem, pos_sem)
    cp.start()
    cp.wait()

    def issue(t, _):
        for k in range(TOP_K):
            _row_copy(hn_ref, t, xs_hbm, pos_smem[k * tm + t], row_sem).start(priority=k % 2)
        return 0

    lax.fori_loop(0, tm, issue, 0)

    def drain(t, _):
        for k in range(TOP_K):
            _row_copy(hn_ref, 0, xs_hbm, 0, row_sem).wait()
        return 0

    lax.fori_loop(0, tm, drain, 0)


def _dispatch(hn, pos_tiles, ends, padded, n_rows):
    tm = TOK_TILE
    T = hn.shape[0] // TOKEN_SUBLANES
    grid_spec = pltpu.PrefetchScalarGridSpec(
        num_scalar_prefetch=2,
        grid=(T // tm,),
        in_specs=[
            pl.BlockSpec((tm * TOKEN_SUBLANES, LANES), lambda i, e, p: (i, 0)),
            pl.BlockSpec(memory_space=pl.ANY),
        ],
        out_specs=pl.BlockSpec(memory_space=pl.ANY),
        scratch_shapes=[
            pltpu.VMEM((EXPERT_TILE * TOKEN_SUBLANES, LANES), F32),
            pltpu.SMEM((TOP_K * tm,), jnp.int32),
            pltpu.SemaphoreType.DMA,
            pltpu.SemaphoreType.DMA,
            pltpu.SemaphoreType.DMA,
        ],
    )
    return pl.pallas_call(
        _dispatch_kernel,
        grid_spec=grid_spec,
        out_shape=jax.ShapeDtypeStruct((n_rows * TOKEN_SUBLANES, LANES), F32),
        compiler_params=_cparams(("arbitrary",)),
        name="dispatch",
    )(ends, padded, hn, pos_tiles)


def _experts_kernel(te_ref, nu_ref, x_ref, wgu_ref, bgu_ref, wdn_ref, bdn_ref, y_ref, wgu_b, wdn_b):
    i = pl.program_id(0)
    F = wdn_ref.shape[1]
    prev = te_ref[jnp.maximum(i - 1, 0)]
    fresh = (i == 0) | (te_ref[i] != prev)

    @pl.when(fresh & (i < nu_ref[0]))
    def _():
        wgu_b[...] = wgu_ref[0].astype(BF16)
        wdn_b[...] = wdn_ref[0].astype(BF16)

    @pl.when(i < nu_ref[0])
    def _():
        xb = _load_token_tiles(x_ref).astype(BF16)
        hu = _dot(xb, wgu_b[...]) + bgu_ref[0]
        glu = jnp.minimum(hu[:, :F], SWIGLU_LIMIT)
        lin = jnp.clip(hu[:, F:], -SWIGLU_LIMIT, SWIGLU_LIMIT)
        act = glu * _sigmoid(SWIGLU_ALPHA * glu) * (lin + 1.0)
        _store_token_tiles(y_ref, _dot(act.astype(BF16), wdn_b[...]) + bdn_ref[0])

    @pl.when(i >= nu_ref[0])
    def _():
        y_ref[...] = jnp.zeros_like(y_ref)


def _experts(tile_expert, n_used, x_sorted, w_gu, b_gu, w_dn, b_dn):
    P = x_sorted.shape[0] // TOKEN_SUBLANES
    E, D, F2 = w_gu.shape
    F = F2 // 2
    tm = EXPERT_TILE
    n_tiles = P // tm

    def row_map(i, te, nu):
        return (jnp.maximum(jnp.minimum(i, nu[0] - 1), 0), 0)

    def exp_map(i, te, nu):
        return (te[i], 0, 0)

    grid_spec = pltpu.PrefetchScalarGridSpec(
        num_scalar_prefetch=2,
        grid=(n_tiles,),
        in_specs=[
            pl.BlockSpec((tm * TOKEN_SUBLANES, LANES), row_map),
            pl.BlockSpec((1, D, F2), exp_map),
            pl.BlockSpec((1, 1, F2), exp_map),
            pl.BlockSpec((1, F, D), exp_map),
            pl.BlockSpec((1, 1, D), exp_map),
        ],
        out_specs=pl.BlockSpec((tm * TOKEN_SUBLANES, LANES), lambda i, te, nu: (i, 0)),
        scratch_shapes=[pltpu.VMEM((D, F2), BF16), pltpu.VMEM((F, D), BF16)],
    )
    return pl.pallas_call(
        _experts_kernel,
        grid_spec=grid_spec,
        out_shape=jax.ShapeDtypeStruct((P * TOKEN_SUBLANES, LANES), F32),
        compiler_params=_cparams(("arbitrary",)),
        name="experts",
    )(tile_expert, n_used, x_sorted, w_gu, b_gu.reshape(E, 1, F2), w_dn, b_dn.reshape(E, 1, D))


def _combine_kernel(x1_ref, wts_ref, p_ref, g_ref, wg_ref, wp_ref, pos_hbm, ys_hbm,
                    out_ref, ybuf, pos_smem, pos_sem, row_sem):
    tm = TOK_TILE
    step = pl.program_id(0)
    n_steps = pl.num_programs(0)
    slot = lax.rem(step, 2)

    def gather(tile, dst_slot):
        cp = pltpu.make_async_copy(pos_hbm.at[tile], pos_smem.at[dst_slot], pos_sem)
        cp.start()
        cp.wait()

        def issue(t, _):
            for k in range(TOP_K):
                _row_copy(ys_hbm, pos_smem[dst_slot, k * tm + t], ybuf.at[dst_slot, k], t,
                          row_sem.at[dst_slot]).start(priority=k % 2)
            return 0

        lax.fori_loop(0, tm, issue, 0)

    @pl.when(step == 0)
    def _():
        gather(0, 0)

    @pl.when(step + 1 < n_steps)
    def _():
        gather(step + 1, 1 - slot)

    def drain(t, _):
        for k in range(TOP_K):
            _row_copy(ys_hbm, 0, ybuf.at[slot, k], 0, row_sem.at[slot]).wait()
        return 0

    lax.fori_loop(0, tm, drain, 0)

    w = wts_ref[...]
    x2 = x1_ref[...]
    for k in range(TOP_K):
        x2 = x2 + w[:, k:k + 1] * _load_token_tiles(ybuf.at[slot, k])
    hn = x2 * lax.rsqrt(jnp.mean(x2 * x2, axis=-1, keepdims=True) + RMS_EPS) * g_ref[...]
    gate = _sigmoid(_dot(hn.astype(BF16), wg_ref[...]))
    out_ref[...] = x2 + gate * _dot(p_ref[...].astype(BF16), wp_ref[...])


def _combine(x1, wts, p2, g, w_gate, w_proj, pos_tiles, y_sorted):
    T, D = x1.shape
    tm = TOK_TILE
    PD = p2.shape[1]
    return pl.pallas_call(
        _combine_kernel,
        grid=(T // tm,),
        in_specs=[
            pl.BlockSpec((tm, D), lambda i: (i, 0)),
            pl.BlockSpec((tm, LANES), lambda i: (i, 0)),
            pl.BlockSpec((tm, PD), lambda i: (i, 0)),
            pl.BlockSpec((1, D), lambda i: (0, 0)),
            pl.BlockSpec((D, D), lambda i: (0, 0)),
            pl.BlockSpec((PD, D), lambda i: (0, 0)),
            pl.BlockSpec(memory_space=pl.ANY),
            pl.BlockSpec(memory_space=pl.ANY),
        ],
        out_specs=pl.BlockSpec((tm, D), lambda i: (i, 0)),
        out_shape=jax.ShapeDtypeStruct((T, D), F32),
        scratch_shapes=[
            pltpu.VMEM((2, TOP_K, tm * TOKEN_SUBLANES, LANES), F32),
            pltpu.SMEM((2, TOP_K * tm), jnp.int32),
            pltpu.SemaphoreType.DMA,
            pltpu.SemaphoreType.DMA((2,)),
        ],
        compiler_params=_cparams(("arbitrary",)),
        name="combine",
    )(x1, wts, p2, g, w_gate, w_proj, pos_tiles, y_sorted)


def _routing_tables(route, counts, n_tokens):
    te = EXPERT_TILE
    tm = TOK_TILE
    idx = route[:TOP_K]
    rank = route[TOP_K:]
    cnt = counts[:, 0]
    padded = ((cnt + te - 1) // te) * te
    ends = jnp.cumsum(padded)
    offs = ends - padded
    e_ids = jnp.arange(N_EXPERTS, dtype=jnp.int32)[:, None, None]
    pos = jnp.sum(jnp.where(idx[None] == e_ids, offs[:, None, None], 0), axis=0) + rank
    pos_tiles = pos.reshape(TOP_K, n_tokens // tm, tm).transpose(1, 0, 2).reshape(n_tokens // tm, TOP_K * tm)
    n_rows = n_tokens * TOP_K + N_EXPERTS * te
    starts = jnp.arange(n_rows // te, dtype=jnp.int32) * te
    tile_expert = jnp.minimum(jnp.sum((starts[:, None] >= ends[None, :]).astype(jnp.int32), axis=1), N_EXPERTS - 1)
    n_used = (ends[-1] // te).astype(jnp.int32).reshape(1)
    return (pos_tiles.astype(jnp.int32), tile_expert.astype(jnp.int32), n_used,
            ends.astype(jnp.int32), padded.astype(jnp.int32), n_rows)


def kernel(x, p, mix_norm_g, w_in, conv_w, conv_b, igate_b, fgate_b, mlstm_head_g, q_norm_g, k_norm_g, w_out, ffn_norm_g, w_router, b_router, w_gate_up, b_gate_up, w_down, b_down, ple_norm_g, w_ple_gate, w_ple_proj):
    B, S, D = x.shape
    T = B * S
    depth = mix_norm_g.shape[0]
    for l in range(depth):
        h_m, h_b = _mixing(x, mix_norm_g[l], w_in[l], conv_w[l], conv_b[l], igate_b[l], fgate_b[l],
                           mlstm_head_g[l], q_norm_g[l], k_norm_g[l])
        b_r = jnp.broadcast_to(b_router[l].reshape(N_EXPERTS, 1), (N_EXPERTS, LANES))
        x1, hn, route, wts, counts = _out_proj(
            x.reshape(T, D), h_m.reshape(T, GROUP_WIDTH), h_b.reshape(T, GROUP_WIDTH),
            w_out[l].astype(BF16), ffn_norm_g[l].reshape(1, D), w_router[l].T, b_r)
        pos_tiles, tile_expert, n_used, ends, padded, n_rows = _routing_tables(route, counts, T)
        x_sorted = _dispatch(hn, pos_tiles, ends, padded, n_rows)
        y_sorted = _experts(tile_expert, n_used, x_sorted, w_gate_up[l], b_gate_up[l], w_down[l], b_down[l])
        out = _combine(x1, wts, p[l].reshape(T, -1), ple_norm_g[l].reshape(1, D),
                       w_ple_gate[l].astype(BF16), w_ple_proj[l].astype(BF16), pos_tiles, y_sorted)
        x = out.reshape(B, S, D)
    return x
```
